```python
import math
import jax, jax.numpy as jnp
from jax import lax
import numpy as np

D_MODEL = 1024
BATCH = 32
SEQ = 256
DEPTH = 2
DEC_BATCH = 2
DEC_SEQ = 4096
PAST_LEN = 256

GRID_W = 64
N_HEADS = 8
N_KV_HEADS = 2
HEAD_DIM = 64
ATTN_WIDTH = N_HEADS * HEAD_DIM
KV_WIDTH = N_KV_HEADS * HEAD_DIM
CONV_WIDTH = D_MODEL // 4
CONV_GROUPS = 4
CONV_KSIZE = 31
HYENA_WIDTH = D_MODEL // 4
HYENA_ORDER = 2
HYENA_SHORT = 3
HYENA_POS_BANDS = 16
HYENA_POS_DIM = 1 + 2 * HYENA_POS_BANDS
HYENA_FILTER_HIDDEN = 64
HYENA_MIN_DECAY = math.log(1e-2) / 1.5
HYENA_MAX_DECAY = math.log(1e-2) / 0.3
MIX_WIDTH = ATTN_WIDTH + CONV_WIDTH + HYENA_WIDTH
IN_WIDTH = ATTN_WIDTH + 2 * KV_WIDTH + 2 * CONV_WIDTH + (HYENA_ORDER + 1) * HYENA_WIDTH
IN_SPLITS = (ATTN_WIDTH, ATTN_WIDTH + KV_WIDTH, ATTN_WIDTH + 2 * KV_WIDTH,
             ATTN_WIDTH + 2 * KV_WIDTH + 2 * CONV_WIDTH)
FFN_DIM = 2816
FFN_KSIZE = 3
ROPE_BASE = 10000.0
NORM_EPS = 1e-6
Q_BLOCK = 128

kernel_name = "hybrid_conv_gqa_hyena_prefix_dit_step"


def _rms_norm(x, w):
    xf = x.astype(jnp.float32)
    y = xf * lax.rsqrt(jnp.mean(xf * xf, axis=-1, keepdims=True) + NORM_EPS)
    return (y * w.astype(jnp.float32)).astype(x.dtype)


def _dwconv(x, w, b):
    pad = (w.shape[0] - 1) // 2
    y = lax.conv_general_dilated(
        x, w[:, None, :].astype(x.dtype), window_strides=(1,), padding=[(pad, pad)],
        dimension_numbers=('NWC', 'WIO', 'NWC'), feature_group_count=x.shape[-1])
    return y + b.astype(x.dtype)


def _rope_1d(x, pos):
    d = x.shape[-1]
    inv = ROPE_BASE ** (-jnp.arange(0, d, 2, dtype=jnp.float32) / d)
    ang = pos.astype(jnp.float32)[:, None] * inv[None, :]
    cos = jnp.cos(ang)[:, None, :]
    sin = jnp.sin(ang)[:, None, :]
    xf = x.astype(jnp.float32)
    x1, x2 = xf[..., : d // 2], xf[..., d // 2:]
    return jnp.concatenate([x1 * cos - x2 * sin, x1 * sin + x2 * cos], axis=-1).astype(x.dtype)


def _rope_2d(x, rows, cols):
    half = x.shape[-1] // 2
    return jnp.concatenate([_rope_1d(x[..., :half], rows), _rope_1d(x[..., half:], cols)], axis=-1)


def _grid_positions(length):
    n_rows = length // GRID_W
    r, col = jnp.meshgrid(jnp.arange(n_rows, dtype=jnp.int32), jnp.arange(GRID_W, dtype=jnp.int32), indexing='ij')
    return r.reshape(-1), col.reshape(-1)


def _blocked_attention(q, k, v):
    b, lq, h, d = q.shape
    g = h // N_KV_HEADS
    nb = lq // Q_BLOCK
    qb = q.reshape(b, nb, Q_BLOCK, N_KV_HEADS, g, d).transpose(1, 0, 2, 3, 4, 5)
    scale = d ** -0.5

    def one_block(qi):
        s = jnp.einsum('bqkgd,bskd->bkgqs', qi, k, preferred_element_type=jnp.float32) * scale
        p = jax.nn.softmax(s, axis=-1).astype(v.dtype)
        return jnp.einsum('bkgqs,bskd->bqkgd', p, v)

    o = lax.map(one_block, qb)
    return o.transpose(1, 0, 2, 3, 4, 5).reshape(b, lq, h * d)


def _conformer_conv(z, p):
    a, gate = jnp.split(z, 2, axis=-1)
    u = a * jax.nn.sigmoid(gate)
    u = _dwconv(u, p['conv_dw_w'], p['conv_dw_b'])
    b, L, C = u.shape
    uf = u.astype(jnp.float32).reshape(b, L, CONV_GROUPS, C // CONV_GROUPS)
    mu = jnp.mean(uf, axis=-1, keepdims=True)
    var = jnp.mean(jnp.square(uf - mu), axis=-1, keepdims=True)
    uf = ((uf - mu) * lax.rsqrt(var + NORM_EPS)).reshape(b, L, C)
    u = (uf * p['conv_gn_w'].astype(jnp.float32) + p['conv_gn_b'].astype(jnp.float32)).astype(z.dtype)
    u = jax.nn.silu(u)
    return u @ p['conv_pw_w'] + p['conv_pw_b']


def _hyena_filter_fft(length, p):
    f32 = jnp.float32
    t = jnp.linspace(0.0, 1.0, length, dtype=f32)[:, None]
    w_ang = 2.0 * math.pi * jnp.arange(length, dtype=f32)[:, None] / length
    bands = jnp.linspace(1e-4, HYENA_POS_BANDS - 1, HYENA_POS_BANDS, dtype=f32)[None, :]
    z = jnp.concatenate([t, jnp.cos(bands * w_ang), jnp.sin(-bands * w_ang)], axis=-1)
    fr = p['hy_freq'].astype(f32)
    hid = jnp.sin(fr * (z @ p['hy_f_w1'].astype(f32) + p['hy_f_b1'].astype(f32)))
    hid = jnp.sin(fr * (hid @ p['hy_f_w2'].astype(f32) + p['hy_f_b2'].astype(f32)))
    h = (hid @ p['hy_f_w3'].astype(f32) + p['hy_f_b3'].astype(f32)).reshape(length, HYENA_ORDER, 2, HYENA_WIDTH)
    deltas = jnp.abs(jnp.linspace(HYENA_MIN_DECAY, HYENA_MAX_DECAY, HYENA_WIDTH, dtype=f32))
    h = h * jnp.exp(-t * deltas[None, :])[:, None, None, :]
    h_fwd, h_bwd = h[:, :, 0], h[:, :, 1]
    two_sided = jnp.concatenate([h_fwd, jnp.zeros_like(h_fwd[:1]), h_bwd[1:][::-1]], axis=0)
    return jnp.fft.rfft(two_sided, axis=0)


def _fft_conv(u, kf, bias):
    L = u.shape[1]
    U = jnp.fft.rfft(u, n=2 * L, axis=1)
    y = jnp.fft.irfft(U * kf[None], n=2 * L, axis=1)[:, :L]
    return y + u * bias


def _hyena(z, p):
    z = _dwconv(z, p['hy_short_w'], p['hy_short_b'])
    v, g1, g2 = jnp.split(z.astype(jnp.float32), 3, axis=-1)
    kf = _hyena_filter_fft(z.shape[1], p)
    bias = p['hy_bias'].astype(jnp.float32)
    u = _fft_conv(v, kf[:, 0], bias[0]) * g1
    u = _fft_conv(u, kf[:, 1], bias[1]) * g2
    return u.astype(z.dtype)


def _layer(x, cond, p, pos, ctx_kv):
    b, L, _ = x.shape
    mod = jax.nn.silu(cond) @ p['w_mod'] + p['b_mod']
    sh1, sc1, g1, sh2, sc2, g2 = jnp.split(mod[:, None, :], 6, axis=-1)

    h = _rms_norm(x, p['norm1']) * (1 + sc1) + sh1
    proj = h @ p['w_in']
    q, k, v, zc, zh = jnp.split(proj, IN_SPLITS, axis=-1)
    q = _rms_norm(q.reshape(b, L, N_HEADS, HEAD_DIM), p['q_norm'])
    k = _rms_norm(k.reshape(b, L, N_KV_HEADS, HEAD_DIM), p['k_norm'])
    v = v.reshape(b, L, N_KV_HEADS, HEAD_DIM)
    if pos is None:
        keys, vals = k, v
    else:
        rows, cols = pos
        q = _rope_2d(q, rows, cols)
        k = _rope_2d(k, rows, cols)
        keys = jnp.concatenate([k, ctx_kv[0].astype(k.dtype)], axis=1)
        vals = jnp.concatenate([v, ctx_kv[1].astype(v.dtype)], axis=1)
    attn_out = _blocked_attention(q, keys, vals)
    conv_out = _conformer_conv(zc, p)
    hy_out = _hyena(zh, p)
    mix = jnp.concatenate([attn_out, conv_out, hy_out], axis=-1) @ p['w_o']
    x = x + g1 * mix

    h2 = _rms_norm(x, p['norm2']) * (1 + sc2) + sh2
    u = _dwconv(h2 @ p['ffn_w_up'], p['ffn_dw_w'], p['ffn_dw_b'])
    val, gate = jnp.split(u, 2, axis=-1)
    x = x + g2 * ((jax.nn.silu(gate) * val) @ p['ffn_w_down'])
    return x, k, v


def setup_inputs(seed: int = 0) -> dict:
    key = jax.random.key(seed)
    ks = iter(jax.random.split(key, 40))
    f32 = jnp.float32

    def nrm(shape, std):
        return jax.random.normal(next(ks), shape, f32) * std

    D, FH, C = D_MODEL, HYENA_FILTER_HIDDEN, HYENA_WIDTH
    return {
        'x_prompt': nrm((BATCH, SEQ, D), 1.0),
        'x_sample': nrm((DEC_BATCH, DEC_SEQ, D), 1.0),
        'cache_k': nrm((DEC_BATCH, DEPTH, PAST_LEN, N_KV_HEADS, HEAD_DIM), 1.0),
        'cache_v': nrm((DEC_BATCH, DEPTH, PAST_LEN, N_KV_HEADS, HEAD_DIM), 1.0),
        'c': nrm((DEC_BATCH, D), 1.0),
        'c_ctx': nrm((D,), 1.0),
        'norm1': 1.0 + nrm((DEPTH, D), 0.02),
        'norm2': 1.0 + nrm((DEPTH, D), 0.02),
        'w_mod': nrm((DEPTH, D, 6 * D), 0.5 * D ** -0.5),
        'b_mod': nrm((DEPTH, 6 * D), 0.02),
        'w_in': nrm((DEPTH, D, IN_WIDTH), D ** -0.5),
        'q_norm': 1.0 + nrm((DEPTH, HEAD_DIM), 0.02),
        'k_norm': 1.0 + nrm((DEPTH, HEAD_DIM), 0.02),
        'conv_dw_w': nrm((DEPTH, CONV_KSIZE, CONV_WIDTH), CONV_KSIZE ** -0.5),
        'conv_dw_b': nrm((DEPTH, CONV_WIDTH), 0.02),
        'conv_gn_w': 1.0 + nrm((DEPTH, CONV_WIDTH), 0.02),
        'conv_gn_b': nrm((DEPTH, CONV_WIDTH), 0.02),
        'conv_pw_w': nrm((DEPTH, CONV_WIDTH, CONV_WIDTH), CONV_WIDTH ** -0.5),
        'conv_pw_b': nrm((DEPTH, CONV_WIDTH), 0.02),
        'hy_short_w': nrm((DEPTH, HYENA_SHORT, (HYENA_ORDER + 1) * C), HYENA_SHORT ** -0.5),
        'hy_short_b': nrm((DEPTH, (HYENA_ORDER + 1) * C), 0.02),
        'hy_f_w1': nrm((DEPTH, HYENA_POS_DIM, FH), HYENA_POS_DIM ** -0.5),
        'hy_f_b1': nrm((DEPTH, FH), 0.1),
        'hy_f_w2': nrm((DEPTH, FH, FH), FH ** -0.5),
        'hy_f_b2': nrm((DEPTH, FH), 0.1),
        'hy_f_w3': nrm((DEPTH, FH, HYENA_ORDER * 2 * C), 0.03 * FH ** -0.5),
        'hy_f_b3': nrm((DEPTH, HYENA_ORDER * 2 * C), 0.005),
        'hy_freq': 1.0 + nrm((DEPTH, FH), 0.1),
        'hy_bias': nrm((DEPTH, HYENA_ORDER, C), 0.5),
        'w_o': nrm((DEPTH, MIX_WIDTH, D), MIX_WIDTH ** -0.5),
        'ffn_w_up': nrm((DEPTH, D, 2 * FFN_DIM), D ** -0.5),
        'ffn_dw_w': nrm((DEPTH, FFN_KSIZE, 2 * FFN_DIM), FFN_KSIZE ** -0.5),
        'ffn_dw_b': nrm((DEPTH, 2 * FFN_DIM), 0.02),
        'ffn_w_down': nrm((DEPTH, FFN_DIM, D), FFN_DIM ** -0.5),
        'final_norm': 1.0 + nrm((D,), 0.02),
    }


def reference(x_prompt, x_sample, cache_k, cache_v, c, c_ctx, norm1, norm2, w_mod, b_mod, w_in,
              q_norm, k_norm, conv_dw_w, conv_dw_b, conv_gn_w, conv_gn_b, conv_pw_w, conv_pw_b,
              hy_short_w, hy_short_b, hy_f_w1, hy_f_b1, hy_f_w2, hy_f_b2, hy_f_w3, hy_f_b3,
              hy_freq, hy_bias, w_o, ffn_w_up, ffn_dw_w, ffn_dw_b, ffn_w_down, final_norm):
    pos = _grid_positions(x_sample.shape[1])
    cond_ctx = c_ctx[None, :]
    xp, xs = x_prompt, x_sample
    ks_new, vs_new = [], []
    for l in range(DEPTH):
        p = {
            'norm1': norm1[l], 'norm2': norm2[l], 'w_mod': w_mod[l], 'b_mod': b_mod[l],
            'w_in': w_in[l], 'q_norm': q_norm[l], 'k_norm': k_norm[l],
            'conv_dw_w': conv_dw_w[l], 'conv_dw_b': conv_dw_b[l], 'conv_gn_w': conv_gn_w[l],
            'conv_gn_b': conv_gn_b[l], 'conv_pw_w': conv_pw_w[l], 'conv_pw_b': conv_pw_b[l],
            'hy_short_w': hy_short_w[l], 'hy_short_b': hy_short_b[l],
            'hy_f_w1': hy_f_w1[l], 'hy_f_b1': hy_f_b1[l], 'hy_f_w2': hy_f_w2[l], 'hy_f_b2': hy_f_b2[l],
            'hy_f_w3': hy_f_w3[l], 'hy_f_b3': hy_f_b3[l], 'hy_freq': hy_freq[l], 'hy_bias': hy_bias[l],
            'w_o': w_o[l], 'ffn_w_up': ffn_w_up[l], 'ffn_dw_w': ffn_dw_w[l], 'ffn_dw_b': ffn_dw_b[l],
            'ffn_w_down': ffn_w_down[l],
        }
        xp, k_ctx, v_ctx = _layer(xp, cond_ctx, p, None, None)
        ks_new.append(k_ctx)
        vs_new.append(v_ctx)
        xs, _, _ = _layer(xs, c, p, pos, (cache_k[:, l], cache_v[:, l]))
    y_prompt = _rms_norm(xp, final_norm)
    y_sample = _rms_norm(xs, final_norm)
    new_k = jnp.stack(ks_new, axis=1)
    new_v = jnp.stack(vs_new, axis=1)
    return (y_prompt, y_sample, new_k, new_v)
```

```python
import functools
import math

import numpy as np
import jax
import jax.numpy as jnp
from jax import lax
from jax.experimental import pallas as pl
from jax.experimental.pallas import tpu as pltpu

F32 = jnp.float32
BF16 = jnp.bfloat16

GRID_W = 64
N_HEADS = 8
N_KV_HEADS = 2
HEAD_DIM = 64
GROUP = N_HEADS // N_KV_HEADS
ATTN_WIDTH = N_HEADS * HEAD_DIM
KV_WIDTH = N_KV_HEADS * HEAD_DIM
CONV_GROUP_WIDTH = 64
CONV_KSIZE = 31
HYENA_ORDER = 2
HYENA_POS_BANDS = 16
HYENA_MIN_DECAY = math.log(1e-2) / 1.5
HYENA_MAX_DECAY = math.log(1e-2) / 0.3
ROPE_BASE = 10000.0
NORM_EPS = 1e-6

LANES = 128
SUBLANES = 8
BF16_SUBLANES = 16
VMEM_LIMIT = 56 * 1024 * 1024

TOKEN_TILE = 512
CONV_TILE = 256
CONV_HALO = 16
ATTN_Q_TILE = 256
ATTN_K_CHUNK = 512
HY_BLOCK = 256
HY_FREQ = HY_BLOCK + 1
HY_FPAD = 264
HY_CTILE = 128
HY_MAC_ROWS = 88
FFN_HALO = 8


def _params(*sem):
    return pltpu.CompilerParams(dimension_semantics=sem, vmem_limit_bytes=VMEM_LIMIT)


def _split(x):
    hi = x.astype(BF16)
    lo = (x - hi.astype(F32)).astype(BF16)
    return hi, lo


def _dot(a, b):
    return jnp.dot(a, b, preferred_element_type=F32)


def _dot3(a, b_hi, b_lo):
    a_hi, a_lo = _split(a)
    return _dot(a_hi, b_hi) + _dot(a_hi, b_lo) + _dot(a_lo, b_hi)


def _dot3l(a_hi, a_lo, b):
    b_hi, b_lo = _split(b)
    return _dot(a_hi, b_hi) + _dot(a_lo, b_hi) + _dot(a_hi, b_lo)


def _sigmoid(x):
    return 1.0 / (1.0 + jnp.exp(-x))


def _mod_kernel(cond_ref, w_ref, b_ref, o_ref):
    c = cond_ref[...]
    s = c * _sigmoid(c)
    o_ref[...] = _dot(s.astype(BF16), w_ref[...].astype(BF16)) + b_ref[...]


def _modulation(cond, w_mod, b_mod):
    depth, d, _ = w_mod.shape
    return pl.pallas_call(
        _mod_kernel,
        grid=(depth, 6),
        in_specs=[
            pl.BlockSpec((SUBLANES, d), lambda l, j: (0, 0)),
            pl.BlockSpec((None, d, d), lambda l, j: (l, 0, j)),
            pl.BlockSpec((None, 1, d), lambda l, j: (l, 0, j)),
        ],
        out_specs=pl.BlockSpec((None, None, SUBLANES, d), lambda l, j: (l, j, 0, 0)),
        out_shape=jax.ShapeDtypeStruct((depth, 6, SUBLANES, d), F32),
        compiler_params=_params("arbitrary", "arbitrary"),
        name="modulation",
    )(cond, w_mod, b_mod.reshape(depth, 1, 6 * d))


def _inproj_kernel(x_ref, mod_ref, n1_ref, w_ref, qn_ref, kn_ref, g_ref, cos_ref, sin_ref,
                   q_ref, k_ref, v_ref, zc_ref, zh_ref, *, n_ctx_tiles):
    is_lat = pl.program_id(0) >= n_ctx_tiles
    x = x_ref[...]
    ms = jnp.mean(x * x, axis=-1, keepdims=True)
    h = x * lax.rsqrt(ms + NORM_EPS) * n1_ref[...]
    h = h * (1.0 + mod_ref[1:2, :]) + mod_ref[0:1, :]
    proj = _dot(h.astype(BF16), w_ref[...])

    g = g_ref[...]
    lane = lax.broadcasted_iota(jnp.int32, (1, LANES), 1)
    low_half = (lane % 32) < 16

    def head_norm(t, w):
        hi, lo = _split(t * t)
        msq = _dot(hi, g) + _dot(lo, g)
        return t * lax.rsqrt(msq + NORM_EPS) * w

    def rope(t):
        partner = jnp.where(low_half, pltpu.roll(t, LANES - 16, axis=1), pltpu.roll(t, 16, axis=1))
        return t * cos_ref[...] + partner * sin_ref[...]

    scale = HEAD_DIM ** -0.5
    for c in range(ATTN_WIDTH // LANES):
        sl = slice(c * LANES, (c + 1) * LANES)
        qn = head_norm(proj[:, sl], qn_ref[...])

        @pl.when(is_lat)
        def _():
            q_ref[:, sl] = (rope(qn) * scale).astype(q_ref.dtype)

        @pl.when(jnp.logical_not(is_lat))
        def _():
            q_ref[:, sl] = (qn * scale).astype(q_ref.dtype)

    kn = head_norm(proj[:, ATTN_WIDTH:ATTN_WIDTH + KV_WIDTH], kn_ref[...])

    @pl.when(is_lat)
    def _():
        k_ref[...] = rope(kn)

    @pl.when(jnp.logical_not(is_lat))
    def _():
        k_ref[...] = kn

    o = ATTN_WIDTH + KV_WIDTH
    v_ref[...] = proj[:, o:o + KV_WIDTH]
    o += KV_WIDTH
    zc_ref[...] = proj[:, o:o + zc_ref.shape[1]].astype(zc_ref.dtype)
    o += zc_ref.shape[1]
    zh_ref[...] = proj[:, o:o + zh_ref.shape[1]].astype(zh_ref.dtype)


def _rope_tables(length):
    pos = np.arange(length)
    rows, cols = pos // GRID_W, pos % GRID_W
    half = HEAD_DIM // 2
    inv = ROPE_BASE ** (-np.arange(0, half, 2, dtype=np.float64) / half)
    d = np.arange(HEAD_DIM)
    p = np.where(d[None, :] < half, rows[:, None], cols[:, None]).astype(np.float64)
    ang = p * inv[d % (half // 2)][None, :]
    sign = np.where((d % half) < half // 2, -1.0, 1.0)[None, :]
    cos = np.tile(np.cos(ang), (1, LANES // HEAD_DIM))
    sin = np.tile(np.sin(ang) * sign, (1, LANES // HEAD_DIM))
    return jnp.asarray(cos, F32), jnp.asarray(sin, F32)


def _group_mean_matrix(width, group):
    idx = np.arange(width) // group
    return jnp.asarray((idx[:, None] == idx[None, :]).astype(np.float32) / group, BF16)


def _cond_index(i, n_ctx_tiles, tiles_per_lat_seq):
    return jnp.where(i < n_ctx_tiles, 0, 1 + (i - n_ctx_tiles) // tiles_per_lat_seq)


def _inproj(x, mod_l, norm1, w_in, q_norm, k_norm, n_ctx, lat_seq, zc_w, zh_w):
    nt, d = x.shape
    tm = TOKEN_TILE
    n_ctx_tiles = n_ctx // tm
    lat_tiles = lat_seq // tm
    cos, sin = _rope_tables(lat_seq)
    g = _group_mean_matrix(LANES, HEAD_DIM)
    cid = functools.partial(_cond_index, n_ctx_tiles=n_ctx_tiles, tiles_per_lat_seq=lat_tiles)
    tab = lambda i: (jnp.maximum(i - n_ctx_tiles, 0) % lat_tiles, 0)
    row = lambda i: (i, 0)
    const = lambda i: (0, 0)
    return pl.pallas_call(
        functools.partial(_inproj_kernel, n_ctx_tiles=n_ctx_tiles),
        grid=(nt // tm,),
        in_specs=[
            pl.BlockSpec((tm, d), row),
            pl.BlockSpec((None, 6, d), lambda i: (cid(i), 0, 0)),
            pl.BlockSpec((1, d), const),
            pl.BlockSpec(w_in.shape, const),
            pl.BlockSpec((1, LANES), const),
            pl.BlockSpec((1, LANES), const),
            pl.BlockSpec((LANES, LANES), const),
            pl.BlockSpec((tm, LANES), tab),
            pl.BlockSpec((tm, LANES), tab),
        ],
        out_specs=[
            pl.BlockSpec((tm, ATTN_WIDTH), row),
            pl.BlockSpec((tm, KV_WIDTH), row),
            pl.BlockSpec((tm, KV_WIDTH), row),
            pl.BlockSpec((tm, zc_w), row),
            pl.BlockSpec((tm, zh_w), row),
        ],
        out_shape=[
            jax.ShapeDtypeStruct((nt, ATTN_WIDTH), BF16),
            jax.ShapeDtypeStruct((nt, KV_WIDTH), F32),
            jax.ShapeDtypeStruct((nt, KV_WIDTH), F32),
            jax.ShapeDtypeStruct((nt, zc_w), BF16),
            jax.ShapeDtypeStruct((nt, zh_w), BF16),
        ],
        compiler_params=_params("parallel"),
        name="inproj",
    )(x, mod_l, norm1.reshape(1, d), w_in,
      jnp.tile(q_norm, LANES // HEAD_DIM).reshape(1, LANES),
      jnp.tile(k_norm, LANES // HEAD_DIM).reshape(1, LANES), g, cos, sin)


def _attn_kernel(qt_ref, k_ref, vt_ref, ot_ref, *, n_full, tail):
    tq = qt_ref.shape[1]
    kc = ATTN_K_CHUNK

    def step(qg, start, size, carry):
        m, l, acc = carry
        k = k_ref[pl.ds(start, size), :]
        s = _dot(k, qg)
        m_new = jnp.maximum(m, jnp.max(s, axis=0, keepdims=True))
        alpha = jnp.exp(m - m_new)
        p = jnp.exp(s - m_new)
        l = alpha * l + jnp.sum(p, axis=0, keepdims=True)
        vt = vt_ref[:, pl.ds(start, size)]
        acc = acc * alpha + _dot(vt, p.astype(BF16))
        return m_new, l, acc

    for g in range(GROUP):
        qg = qt_ref[g * HEAD_DIM:(g + 1) * HEAD_DIM, :]
        carry = (jnp.full((1, tq), -1e30, F32), jnp.zeros((1, tq), F32), jnp.zeros((HEAD_DIM, tq), F32))
        if n_full:
            carry = lax.fori_loop(
                0, n_full, lambda i, c: step(qg, pl.multiple_of(i * kc, kc), kc, c), carry)
        if tail:
            carry = step(qg, n_full * kc, tail, carry)
        _, l, acc = carry
        ot_ref[g * HEAD_DIM:(g + 1) * HEAD_DIM, :] = (acc / l).astype(ot_ref.dtype)


def _attention(qt, k, vt, tok_off, seq):
    nb, _, lk, _ = k.shape
    tq = ATTN_Q_TILE
    nq = seq // tq
    off = tok_off // tq
    rows = GROUP * HEAD_DIM
    return pl.pallas_call(
        functools.partial(_attn_kernel, n_full=lk // ATTN_K_CHUNK, tail=lk % ATTN_K_CHUNK),
        grid=(nb, N_KV_HEADS, nq),
        in_specs=[
            pl.BlockSpec((rows, tq), lambda b, h, i: (h, off + b * nq + i)),
            pl.BlockSpec((None, None, lk, HEAD_DIM), lambda b, h, i: (b, h, 0, 0)),
            pl.BlockSpec((None, None, HEAD_DIM, lk), lambda b, h, i: (b, h, 0, 0)),
        ],
        out_specs=pl.BlockSpec((rows, tq), lambda b, h, i: (h, b * nq + i)),
        out_shape=jax.ShapeDtypeStruct((ATTN_WIDTH, nb * seq), BF16),
        compiler_params=_params("parallel", "parallel", "parallel"),
        name="attention",
    )(qt, k, vt)


def _conformer_kernel(z_ref, zp_ref, zn_ref, dww_ref, dwb_ref, gnw_ref, gnb_ref, g_ref, pww_ref, pwb_ref,
                      o_ref, ubuf_ref, *, n_ctx_tiles, tiles_per_lat_seq):
    i = pl.program_id(0)
    tt, c = o_ref.shape
    halo = CONV_HALO
    is_ctx = i < n_ctx_tiles
    j = (i - n_ctx_tiles) % tiles_per_lat_seq
    first = jnp.logical_or(is_ctx, j == 0)
    last = jnp.logical_or(is_ctx, j == tiles_per_lat_seq - 1)

    def glu(ref):
        z = ref[...].astype(F32)
        return z[:, :c] * _sigmoid(z[:, c:])

    ubuf_ref[0:halo, :] = jnp.where(first, 0.0, glu(zp_ref))
    ubuf_ref[halo:halo + tt, :] = glu(z_ref)
    ubuf_ref[halo + tt:, :] = jnp.where(last, 0.0, glu(zn_ref))

    pad = (CONV_KSIZE - 1) // 2
    rows = 64
    g = g_ref[...]
    for r in range(tt // rows):
        base = halo + r * rows - pad
        acc = jnp.zeros((rows, c), F32) + dwb_ref[...]
        for k in range(CONV_KSIZE):
            acc = acc + ubuf_ref[base + k:base + k + rows, :] * dww_ref[k:k + 1, :]
        hi, lo = _split(acc)
        mu = _dot(hi, g) + _dot(lo, g)
        dlt = acc - mu
        hi, lo = _split(dlt * dlt)
        var = _dot(hi, g) + _dot(lo, g)
        un = dlt * lax.rsqrt(var + NORM_EPS) * gnw_ref[...] + gnb_ref[...]
        act = un * _sigmoid(un)
        out = _dot(act.astype(BF16), pww_ref[...]) + pwb_ref[...]
        o_ref[r * rows:(r + 1) * rows, :] = out.astype(o_ref.dtype)


def _conformer(zc, dw_w, dw_b, gn_w, gn_b, pw_w, pw_b, n_ctx, ctx_seq, lat_seq):
    nt, c2 = zc.shape
    c = c2 // 2
    tt = CONV_TILE
    assert ctx_seq == tt
    hb = tt // CONV_HALO
    n_halo_blocks = nt // CONV_HALO
    dw_w = jnp.concatenate([dw_w, jnp.zeros((1, c), F32)], axis=0)
    g = _group_mean_matrix(c, CONV_GROUP_WIDTH)
    const = lambda i: (0, 0)
    return pl.pallas_call(
        functools.partial(_conformer_kernel, n_ctx_tiles=n_ctx // tt, tiles_per_lat_seq=lat_seq // tt),
        grid=(nt // tt,),
        in_specs=[
            pl.BlockSpec((tt, c2), lambda i: (i, 0)),
            pl.BlockSpec((CONV_HALO, c2), lambda i: (jnp.maximum(i * hb - 1, 0), 0)),
            pl.BlockSpec((CONV_HALO, c2), lambda i: (jnp.minimum((i + 1) * hb, n_halo_blocks - 1), 0)),
            pl.BlockSpec((CONV_KSIZE + 1, c), const),
            pl.BlockSpec((1, c), const),
            pl.BlockSpec((1, c), const),
            pl.BlockSpec((1, c), const),
            pl.BlockSpec((c, c), const),
            pl.BlockSpec((c, c), const),
            pl.BlockSpec((1, c), const),
        ],
        out_specs=pl.BlockSpec((tt, c), lambda i: (i, 0)),
        out_shape=jax.ShapeDtypeStruct((nt, c), BF16),
        scratch_shapes=[pltpu.VMEM((tt + 2 * CONV_HALO, c), F32)],
        compiler_params=_params("parallel"),
        name="conformer",
    )(zc, zc, zc, dw_w, dw_b.reshape(1, c), gn_w.reshape(1, c), gn_b.reshape(1, c), g,
      pw_w.astype(BF16), pw_b.reshape(1, c))


def _dft_tables():
    t = np.arange(HY_BLOCK)
    k = np.arange(HY_FPAD)
    ang = ((k[:, None] * t[None, :]) % (2 * HY_BLOCK)) * (2.0 * np.pi / (2 * HY_BLOCK))
    live = (k < HY_FREQ)[:, None]
    fwd = np.concatenate([np.where(live, np.cos(ang), 0.0), np.where(live, -np.sin(ang), 0.0)], axis=0)
    ck = np.where((k == 0) | (k == HY_BLOCK), 1.0, 2.0)[:, None] / (2 * HY_BLOCK)
    inv = np.concatenate([np.where(live, ck * np.cos(ang), 0.0), np.where(live, -ck * np.sin(ang), 0.0)], axis=0).T

    def pair(a):
        a = jnp.asarray(a, F32)
        hi = a.astype(BF16)
        return hi, (a - hi.astype(F32)).astype(BF16)

    return pair(fwd), pair(inv)


def _filter_features(length):
    lag = np.arange(2 * length) - length
    idx = np.minimum(np.abs(lag), length - 1)
    t = np.linspace(0.0, 1.0, length)[idx]
    w_ang = 2.0 * np.pi * idx / length
    bands = np.linspace(1e-4, HYENA_POS_BANDS - 1, HYENA_POS_BANDS)
    z = np.zeros((2 * length, LANES), np.float64)
    z[:, 0] = t
    z[:, 1:1 + HYENA_POS_BANDS] = np.cos(bands[None, :] * w_ang[:, None])
    z[:, 1 + HYENA_POS_BANDS:1 + 2 * HYENA_POS_BANDS] = np.sin(-bands[None, :] * w_ang[:, None])
    z[:, 1 + 2 * HYENA_POS_BANDS] = (lag > -length).astype(np.float64)
    return jnp.asarray(z, F32)


def _hyena_filter_kernel(z_ref, w1_ref, b1_ref, w2_ref, b2_ref, fr_ref, w3_ref, b3_ref, dl_ref,
                         fh_ref, fl_ref, h_ref, prev_ref):
    s = pl.program_id(0)
    z = z_ref[...]
    fr = fr_ref[...]

    def dense(a, w_ref_, b_ref_):
        w_hi, w_lo = _split(w_ref_[...])
        return _dot3(a, w_hi, w_lo) + b_ref_[...]

    hid = jnp.sin(fr * dense(z, w1_ref, b1_ref))
    hid = jnp.sin(fr * dense(hid, w2_ref, b2_ref))
    h = dense(hid, w3_ref, b3_ref)
    t = z[:, 0:1]
    live = z[:, 1 + 2 * HYENA_POS_BANDS:2 + 2 * HYENA_POS_BANDS]
    h = h * jnp.exp(-t * dl_ref[...]) * live
    spec = _dot3l(fh_ref[...], fl_ref[...], h)

    @pl.when(s == 0)
    def _():
        prev_ref[...] = jnp.zeros_like(prev_ref)

    row = lax.broadcasted_iota(jnp.int32, (2 * HY_FPAD, 1), 0)
    sign = (1 - 2 * ((row % HY_FPAD) % 2)).astype(F32)
    h_ref[...] = spec + sign * prev_ref[...]
    prev_ref[...] = spec


def _hyena_filters(length, w1, b1, w2, b2, w3, b3, freq, dft_fwd):
    nb = length // HY_BLOCK
    fh = w2.shape[0]
    oc = w3.shape[1] // 2
    c = oc // HYENA_ORDER
    z = _filter_features(length)
    w1p = jnp.concatenate([w1, jnp.zeros((LANES - w1.shape[0], fh), F32)], axis=0)
    w3d = w3.reshape(fh, HYENA_ORDER, 2, c).transpose(2, 0, 1, 3).reshape(2, fh, oc)
    b3d = b3.reshape(HYENA_ORDER, 2, c).transpose(1, 0, 2).reshape(2, 1, oc)
    deltas = np.abs(np.linspace(HYENA_MIN_DECAY, HYENA_MAX_DECAY, c))
    dl = jnp.asarray(np.tile(deltas, HYENA_ORDER)[None, :], F32)
    direction = lambda s: (jnp.where(s >= nb, 0, 1), 0, 0)
    const = lambda s: (0, 0)
    return pl.pallas_call(
        _hyena_filter_kernel,
        grid=(2 * nb,),
        in_specs=[
            pl.BlockSpec((HY_BLOCK, LANES), lambda s: (s, 0)),
            pl.BlockSpec((LANES, fh), const),
            pl.BlockSpec((1, fh), const),
            pl.BlockSpec((fh, fh), const),
            pl.BlockSpec((1, fh), const),
            pl.BlockSpec((1, fh), const),
            pl.BlockSpec((None, fh, oc), direction),
            pl.BlockSpec((None, 1, oc), direction),
            pl.BlockSpec((1, oc), const),
            pl.BlockSpec((2 * HY_FPAD, HY_BLOCK), const),
            pl.BlockSpec((2 * HY_FPAD, HY_BLOCK), const),
        ],
        out_specs=pl.BlockSpec((None, 2 * HY_FPAD, oc), lambda s: (s, 0, 0)),
        out_shape=jax.ShapeDtypeStruct((2 * nb, 2 * HY_FPAD, oc), F32),
        scratch_shapes=[pltpu.VMEM((2 * HY_FPAD, oc), F32)],
        compiler_params=_params("arbitrary"),
        name="hyena_filter",
    )(z, w1p, b1.reshape(1, fh), w2, b2.reshape(1, fh), freq.reshape(1, fh), w3d, b3d, dl,
      dft_fwd[0], dft_fwd[1])


def _hyena_conv_kernel(x_ref, gz_ref, sw_ref, sb_ref, gw_ref, gb_ref, bias_ref, h_ref,
                       fh_ref, fl_ref, gh_ref, gl_ref, o_ref, xs_ref, u_ref, y_ref, *, nb, short_signal):
    length = x_ref.shape[0]
    t_blk = HY_BLOCK
    row = lax.broadcasted_iota(jnp.int32, (length, 1), 0)

    def short_conv(ref, w_ref_, b_ref_):
        a = ref[...].astype(F32)
        prev = jnp.where(row == 0, 0.0, pltpu.roll(a, 1, axis=0))
        nxt = jnp.where(row == length - 1, 0.0, pltpu.roll(a, length - 1, axis=0))
        return prev * w_ref_[0:1, :] + a * w_ref_[1:2, :] + nxt * w_ref_[2:3, :] + b_ref_[...]

    if short_signal:
        xs_ref[...] = short_conv(x_ref, sw_ref, sb_ref)
    else:
        xs_ref[...] = x_ref[...].astype(F32)

    def fwd(j, _):
        xj = xs_ref[pl.ds(pl.multiple_of(j * t_blk, t_blk), t_blk), :]
        u_ref[j] = _dot3l(fh_ref[...], fl_ref[...], xj)
        return 0

    lax.fori_loop(0, nb, fwd, 0)

    rc = HY_MAC_ROWS

    def mac(i, _):
        for r in range(HY_FPAD // rc):
            re = pl.ds(r * rc, rc)
            im = pl.ds(HY_FPAD + r * rc, rc)

            def body(j, acc):
                are, aim = acc
                d = i - j + nb
                hre, him = h_ref[d, re, :], h_ref[d, im, :]
                ure, uim = u_ref[j, re, :], u_ref[j, im, :]
                return are + hre * ure - him * uim, aim + hre * uim + him * ure

            zero = jnp.zeros((rc, x_ref.shape[1]), F32)
            are, aim = lax.fori_loop(0, nb, body, (zero, zero))
            y_ref[i, re, :] = are
            y_ref[i, im, :] = aim
        return 0

    lax.fori_loop(0, nb, mac, 0)

    gate = short_conv(gz_ref, gw_ref, gb_ref)

    def inv(i, _):
        rows = pl.ds(pl.multiple_of(i * t_blk, t_blk), t_blk)
        conv = _dot3l(gh_ref[...], gl_ref[...], y_ref[i])
        xi = xs_ref[rows, :]
        xs_ref[rows, :] = conv + xi * bias_ref[...]
        return 0

    lax.fori_loop(0, nb, inv, 0)
    o_ref[...] = (xs_ref[...] * gate).astype(o_ref.dtype)


def _hyena_conv(x, x_col, zh, gate_col, short_w, short_b, bias, spectra, order, row_off, length, nseq,
                dft_fwd, dft_inv, out_dtype, short_signal):
    c = bias.shape[0]
    ct = HY_CTILE
    nct = c // ct
    nb = length // HY_BLOCK
    assert row_off % length == 0
    rb = row_off // length
    x_rb = rb if x.shape[0] == zh.shape[0] else 0
    const = lambda ci, b: (0, 0)
    return pl.pallas_call(
        functools.partial(_hyena_conv_kernel, nb=nb, short_signal=short_signal),
        grid=(nct, nseq),
        in_specs=[
            pl.BlockSpec((length, ct), lambda ci, b: (x_rb + b, x_col * nct + ci)),
            pl.BlockSpec((length, ct), lambda ci, b: (rb + b, gate_col * nct + ci)),
            pl.BlockSpec((3, ct), lambda ci, b: (0, x_col * nct + ci)),
            pl.BlockSpec((1, ct), lambda ci, b: (0, x_col * nct + ci)),
            pl.BlockSpec((3, ct), lambda ci, b: (0, gate_col * nct + ci)),
            pl.BlockSpec((1, ct), lambda ci, b: (0, gate_col * nct + ci)),
            pl.BlockSpec((1, ct), lambda ci, b: (0, ci)),
            pl.BlockSpec((2 * nb, 2 * HY_FPAD, ct), lambda ci, b: (0, 0, order * nct + ci),
                         pipeline_mode=pl.Buffered(1)),
            pl.BlockSpec((2 * HY_FPAD, HY_BLOCK), const),
            pl.BlockSpec((2 * HY_FPAD, HY_BLOCK), const),
            pl.BlockSpec((HY_BLOCK, 2 * HY_FPAD), const),
            pl.BlockSpec((HY_BLOCK, 2 * HY_FPAD), const),
        ],
        out_specs=pl.BlockSpec((length, ct), lambda ci, b: (b, ci)),
        out_shape=jax.ShapeDtypeStruct((nseq * length, c), out_dtype),
        scratch_shapes=[
            pltpu.VMEM((length, ct), F32),
            pltpu.VMEM((nb, 2 * HY_FPAD, ct), F32),
            pltpu.VMEM((nb, 2 * HY_FPAD, ct), F32),
        ],
        compiler_params=_params("arbitrary", "arbitrary"),
        name="hyena_conv",
    )(x, zh, short_w, short_b.reshape(1, -1), short_w, short_b.reshape(1, -1), bias.reshape(1, c), spectra,
      dft_fwd[0], dft_fwd[1], dft_inv[0], dft_inv[1])


def _hyena(zh, short_w, short_b, hy_bias, spectra, row_off, length, nseq, dft_fwd, dft_inv):
    u1 = _hyena_conv(zh, 0, zh, 1, short_w, short_b, hy_bias[0], spectra, 0, row_off, length, nseq,
                     dft_fwd, dft_inv, F32, True)
    return _hyena_conv(u1, 0, zh, 2, short_w, short_b, hy_bias[1], spectra, 1, row_off, length, nseq,
                       dft_fwd, dft_inv, BF16, False)


def _outproj_kernel(x_ref, a_ref, c_ref, h_ref, mod_ref, wa_ref, wc_ref, wh_ref, o_ref):
    mix = _dot(a_ref[...], wa_ref[...]) + _dot(c_ref[...], wc_ref[...]) + _dot(h_ref[...], wh_ref[...])
    o_ref[...] = x_ref[...] + mod_ref[2:3, :] * mix


def _outproj(x, attn, conv, hy, mod_l, w_o, n_ctx, lat_seq):
    nt, d = x.shape
    tm = TOKEN_TILE
    cid = functools.partial(_cond_index, n_ctx_tiles=n_ctx // tm, tiles_per_lat_seq=lat_seq // tm)
    wa, wc, wh = w_o[:attn.shape[1]], w_o[attn.shape[1]:attn.shape[1] + conv.shape[1]], w_o[-hy.shape[1]:]
    row = lambda i: (i, 0)
    const = lambda i: (0, 0)
    return pl.pallas_call(
        _outproj_kernel,
        grid=(nt // tm,),
        in_specs=[
            pl.BlockSpec((tm, d), row),
            pl.BlockSpec((tm, attn.shape[1]), row),
            pl.BlockSpec((tm, conv.shape[1]), row),
            pl.BlockSpec((tm, hy.shape[1]), row),
            pl.BlockSpec((None, 6, d), lambda i: (cid(i), 0, 0)),
            pl.BlockSpec(wa.shape, const),
            pl.BlockSpec(wc.shape, const),
            pl.BlockSpec(wh.shape, const),
        ],
        out_specs=pl.BlockSpec((tm, d), row),
        out_shape=jax.ShapeDtypeStruct((nt, d), F32),
        compiler_params=_params("parallel"),
        name="outproj",
    )(x, attn, conv, hy, mod_l, wa, wc, wh)


def _ffn_kernel(x_ref, xp_ref, xn_ref, mod_ref, n2_ref, wup_ref, dww_ref, dwb_ref, wdn_ref, fn_ref,
                o_ref, pv_ref, pg_ref, acc_ref, *, n_ctx_tiles, ctx_seq, lat_seq, n_chunks, final):
    i = pl.program_id(0)
    tm, d = x_ref.shape
    ffn = wdn_ref.shape[0]
    nc = ffn // n_chunks
    halo = FFN_HALO

    def norm_mod(x):
        ms = jnp.mean(x * x, axis=-1, keepdims=True)
        h = x * lax.rsqrt(ms + NORM_EPS) * n2_ref[...]
        return h * (1.0 + mod_ref[4:5, :]) + mod_ref[3:4, :]

    hext = jnp.concatenate([norm_mod(xp_ref[...]), norm_mod(x_ref[...]), norm_mod(xn_ref[...])], axis=0)
    hb = hext.astype(BF16)

    seq = jnp.where(i < n_ctx_tiles, ctx_seq, lat_seq)
    pos = (i * tm + lax.broadcasted_iota(jnp.int32, (tm, 1), 0)) & (seq - 1)
    at_start = pos == 0
    at_end = pos == seq - 1

    def dwconv(p_ref, col):
        w = dww_ref[:, col:col + nc]
        prev = jnp.where(at_start, 0.0, p_ref[halo - 1:halo - 1 + tm, :])
        nxt = jnp.where(at_end, 0.0, p_ref[halo + 1:halo + 1 + tm, :])
        return prev * w[0:1] + p_ref[halo:halo + tm, :] * w[1:2] + nxt * w[2:3] + dwb_ref[:, col:col + nc]

    for c in range(n_chunks):
        c0 = c * nc
        pv_ref[...] = _dot(hb, wup_ref[:, c0:c0 + nc])
        pg_ref[...] = _dot(hb, wup_ref[:, ffn + c0:ffn + c0 + nc])
        val = dwconv(pv_ref, c0)
        gate = dwconv(pg_ref, ffn + c0)
        act = (gate * _sigmoid(gate) * val).astype(BF16)
        part = _dot(act, wdn_ref[c0:c0 + nc, :])
        if c == 0:
            acc_ref[...] = part
        else:
            acc_ref[...] += part

    out = x_ref[...] + mod_ref[5:6, :] * acc_ref[...]
    if final:
        ms = jnp.mean(out * out, axis=-1, keepdims=True)
        out = out * lax.rsqrt(ms + NORM_EPS) * fn_ref[...]
    o_ref[...] = out


def _ffn(x, mod_l, norm2, w_up, dw_w, dw_b, w_down, final_norm, n_ctx, ctx_seq, lat_seq, final):
    nt, d = x.shape
    tm = TOKEN_TILE
    ffn = w_down.shape[0]
    n_chunks = 2
    nc = ffn // n_chunks
    hb = tm // FFN_HALO
    n_halo_blocks = nt // FFN_HALO
    cid = functools.partial(_cond_index, n_ctx_tiles=n_ctx // tm, tiles_per_lat_seq=lat_seq // tm)
    const = lambda i: (0, 0)
    single = pl.Buffered(1)
    return pl.pallas_call(
        functools.partial(_ffn_kernel, n_ctx_tiles=n_ctx // tm, ctx_seq=ctx_seq, lat_seq=lat_seq,
                          n_chunks=n_chunks, final=final),
        grid=(nt // tm,),
        in_specs=[
            pl.BlockSpec((tm, d), lambda i: (i, 0)),
            pl.BlockSpec((FFN_HALO, d), lambda i: (jnp.maximum(i * hb - 1, 0), 0)),
            pl.BlockSpec((FFN_HALO, d), lambda i: (jnp.minimum((i + 1) * hb, n_halo_blocks - 1), 0)),
            pl.BlockSpec((None, 6, d), lambda i: (cid(i), 0, 0)),
            pl.BlockSpec((1, d), const),
            pl.BlockSpec(w_up.shape, const, pipeline_mode=single),
            pl.BlockSpec((3, 2 * ffn), const),
            pl.BlockSpec((1, 2 * ffn), const),
            pl.BlockSpec(w_down.shape, const, pipeline_mode=single),
            pl.BlockSpec((1, d), const),
        ],
        out_specs=pl.BlockSpec((tm, d), lambda i: (i, 0)),
        out_shape=jax.ShapeDtypeStruct((nt, d), F32),
        scratch_shapes=[
            pltpu.VMEM((tm + 2 * FFN_HALO, nc), F32),
            pltpu.VMEM((tm + 2 * FFN_HALO, nc), F32),
            pltpu.VMEM((tm, d), F32),
        ],
        compiler_params=_params("parallel"),
        name="convffn",
    )(x, x, x, mod_l, norm2.reshape(1, d), w_up, dw_w, dw_b.reshape(1, -1), w_down, final_norm.reshape(1, d))


def kernel(x_prompt, x_sample, cache_k, cache_v, c, c_ctx, norm1, norm2, w_mod, b_mod, w_in, q_norm, k_norm,
           conv_dw_w, conv_dw_b, conv_gn_w, conv_gn_b, conv_pw_w, conv_pw_b, hy_short_w, hy_short_b,
           hy_f_w1, hy_f_b1, hy_f_w2, hy_f_b2, hy_f_w3, hy_f_b3, hy_freq, hy_bias, w_o, ffn_w_up,
           ffn_dw_w, ffn_dw_b, ffn_w_down, final_norm):
    batch, ctx_seq, d = x_prompt.shape
    dec_batch, lat_seq, _ = x_sample.shape
    depth = norm1.shape[0]
    n_ctx = batch * ctx_seq
    n_lat = dec_batch * lat_seq
    conv_w = conv_pw_w.shape[1]
    hy_w = hy_bias.shape[2]
    assert 1 + dec_batch <= SUBLANES and n_ctx % TOKEN_TILE == 0 and lat_seq % TOKEN_TILE == 0

    x = jnp.concatenate([x_prompt.reshape(n_ctx, d), x_sample.reshape(n_lat, d)], axis=0)
    cond = jnp.concatenate([c_ctx[None, :], c, jnp.zeros((SUBLANES - 1 - dec_batch, d), F32)], axis=0)
    mod = _modulation(cond, w_mod, b_mod).transpose(0, 2, 1, 3)
    dft_fwd, dft_inv = _dft_tables()

    new_k, new_v = [], []
    for l in range(depth):
        q, k, v, zc, zh = _inproj(x, mod[l], norm1[l], w_in[l].astype(BF16), q_norm[l], k_norm[l],
                                  n_ctx, lat_seq, 2 * conv_w, (HYENA_ORDER + 1) * hy_w)
        k_ctx = k[:n_ctx].reshape(batch, ctx_seq, N_KV_HEADS, HEAD_DIM)
        v_ctx = v[:n_ctx].reshape(batch, ctx_seq, N_KV_HEADS, HEAD_DIM)
        new_k.append(k_ctx)
        new_v.append(v_ctx)
        k_lat = jnp.concatenate([k[n_ctx:].reshape(dec_batch, lat_seq, N_KV_HEADS, HEAD_DIM), cache_k[:, l]], axis=1)
        v_lat = jnp.concatenate([v[n_ctx:].reshape(dec_batch, lat_seq, N_KV_HEADS, HEAD_DIM), cache_v[:, l]], axis=1)

        qt = q.T
        at_ctx = _attention(qt, k_ctx.transpose(0, 2, 1, 3).astype(BF16),
                            v_ctx.transpose(0, 2, 3, 1).astype(BF16), 0, ctx_seq)
        at_lat = _attention(qt, k_lat.transpose(0, 2, 1, 3).astype(BF16),
                            v_lat.transpose(0, 2, 3, 1).astype(BF16), n_ctx, lat_seq)
        attn = jnp.concatenate([at_ctx, at_lat], axis=1).T

        conv = _conformer(zc, conv_dw_w[l], conv_dw_b[l], conv_gn_w[l], conv_gn_b[l], conv_pw_w[l],
                          conv_pw_b[l], n_ctx, ctx_seq, lat_seq)

        filt = (hy_f_w1[l], hy_f_b1[l], hy_f_w2[l], hy_f_b2[l], hy_f_w3[l], hy_f_b3[l], hy_freq[l], dft_fwd)
        hy_ctx = _hyena(zh, hy_short_w[l], hy_short_b[l], hy_bias[l], _hyena_filters(ctx_seq, *filt),
                        0, ctx_seq, batch, dft_fwd, dft_inv)
        hy_lat = _hyena(zh, hy_short_w[l], hy_short_b[l], hy_bias[l], _hyena_filters(lat_seq, *filt),
                        n_ctx, lat_seq, dec_batch, dft_fwd, dft_inv)
        hy = jnp.concatenate([hy_ctx, hy_lat], axis=0)

        x = _outproj(x, attn, conv, hy, mod[l], w_o[l].astype(BF16), n_ctx, lat_seq)
        x = _ffn(x, mod[l], norm2[l], ffn_w_up[l].astype(BF16), ffn_dw_w[l], ffn_dw_b[l],
                 ffn_w_down[l].astype(BF16), final_norm, n_ctx, ctx_seq, lat_seq, l == depth - 1)

    y_prompt = x[:n_ctx].reshape(batch, ctx_seq, d)
    y_sample = x[n_ctx:].reshape(dec_batch, lat_seq, d)
    return y_prompt, y_sample, jnp.stack(new_k, axis=1), jnp.stack(new_v, axis=1)
```

```python
import functools
import math

import numpy as np
import jax
import jax.numpy as jnp
from jax import lax
from jax.experimental import pallas as pl
from jax.experimental.pallas import tpu as pltpu

F32 = jnp.float32
BF16 = jnp.bfloat16

GRID_W = 64
N_HEADS = 8
N_KV_HEADS = 2
HEAD_DIM = 64
GROUP = N_HEADS // N_KV_HEADS
ATTN_WIDTH = N_HEADS * HEAD_DIM
KV_WIDTH = N_KV_HEADS * HEAD_DIM
CONV_GROUP_WIDTH = 64
CONV_KSIZE = 31
HYENA_ORDER = 2
HYENA_POS_BANDS = 16
HYENA_MIN_DECAY = math.log(1e-2) / 1.5
HYENA_MAX_DECAY = math.log(1e-2) / 0.3
ROPE_BASE = 10000.0
NORM_EPS = 1e-6

LANES = 128
SUBLANES = 8
BF16_SUBLANES = 16
VMEM_LIMIT = 56 * 1024 * 1024

TOKEN_TILE = 512
CONV_TILE = 256
CONV_HALO = 16
ATTN_Q_TILE = 256
ATTN_K_CHUNK = 512
HY_MAX_BLOCK = 512
HY_SHORT_ROWS = 512
FFN_HALO = 8
FFN_CHUNKS = 2

_SINGLE = pl.Buffered(1)


def _params(*sem):
    return pltpu.CompilerParams(dimension_semantics=sem, vmem_limit_bytes=VMEM_LIMIT)


def _split(x):
    hi = x.astype(BF16)
    lo = (x - hi.astype(F32)).astype(BF16)
    return hi, lo


def _dot(a, b):
    return jnp.dot(a, b, preferred_element_type=F32)


def _dot3(a, b_hi, b_lo):
    a_hi, a_lo = _split(a)
    return _dot(a_hi, b_hi) + _dot(a_hi, b_lo) + _dot(a_lo, b_hi)


def _dot3l(a_hi, a_lo, b):
    b_hi, b_lo = _split(b)
    return _dot(a_hi, b_hi) + _dot(a_lo, b_hi) + _dot(a_hi, b_lo)


def _sigmoid(x):
    return 1.0 / (1.0 + jnp.exp(-x))


def _const(*_):
    return (0, 0)


def _mod_kernel(cond_ref, w_ref, b_ref, o_ref):
    c = cond_ref[...]
    s = c * _sigmoid(c)
    o_ref[...] = _dot(s.astype(BF16), w_ref[...].astype(BF16)) + b_ref[...]


def _modulation(cond, w_mod, b_mod):
    depth, d, _ = w_mod.shape
    return pl.pallas_call(
        _mod_kernel,
        grid=(depth, 6),
        in_specs=[
            pl.BlockSpec((SUBLANES, d), lambda l, j: (0, 0)),
            pl.BlockSpec((None, d, d), lambda l, j: (l, 0, j)),
            pl.BlockSpec((None, 1, d), lambda l, j: (l, 0, j)),
        ],
        out_specs=pl.BlockSpec((None, None, SUBLANES, d), lambda l, j: (l, j, 0, 0)),
        out_shape=jax.ShapeDtypeStruct((depth, 6, SUBLANES, d), F32),
        compiler_params=_params("arbitrary", "arbitrary"),
        name="modulation",
    )(cond, w_mod, b_mod.reshape(depth, 1, 6 * d))


def _mod_spec(d, tile, seq, cond_base):
    per_seq = seq // tile if seq >= tile else 0
    if cond_base == 0:
        return pl.BlockSpec((None, 6, d), lambda i: (0, 0, 0))
    return pl.BlockSpec((None, 6, d), lambda i: (cond_base + i // per_seq, 0, 0))


def _inproj_kernel(x_ref, mod_ref, n1_ref, w_ref, qn_ref, kn_ref, g_ref, cos_ref, sin_ref,
                   qt_ref, k_ref, kb_ref, v_ref, vt_ref, zc_ref, zh_ref, *, rotary):
    x = x_ref[...]
    ms = jnp.mean(x * x, axis=-1, keepdims=True)
    h = x * lax.rsqrt(ms + NORM_EPS) * n1_ref[...]
    h = h * (1.0 + mod_ref[1:2, :]) + mod_ref[0:1, :]
    proj = _dot(h.astype(BF16), w_ref[...])

    g = g_ref[...]
    lane = lax.broadcasted_iota(jnp.int32, (1, LANES), 1)
    low_half = (lane % 32) < 16

    def head_norm(t, w):
        hi, lo = _split(t * t)
        msq = _dot(hi, g) + _dot(lo, g)
        t = t * lax.rsqrt(msq + NORM_EPS) * w
        if rotary:
            partner = jnp.where(low_half, pltpu.roll(t, LANES - 16, axis=1), pltpu.roll(t, 16, axis=1))
            t = t * cos_ref[...] + partner * sin_ref[...]
        return t

    scale = HEAD_DIM ** -0.5 * math.log2(math.e)
    for c in range(ATTN_WIDTH // LANES):
        sl = slice(c * LANES, (c + 1) * LANES)
        qn = head_norm(proj[:, sl], qn_ref[...]) * scale
        qt_ref[sl, :] = qn.T.astype(qt_ref.dtype)

    kn = head_norm(proj[:, ATTN_WIDTH:ATTN_WIDTH + KV_WIDTH], kn_ref[...])
    k_ref[...] = kn
    kb_ref[...] = kn.astype(kb_ref.dtype)
    o = ATTN_WIDTH + KV_WIDTH
    v = proj[:, o:o + KV_WIDTH]
    v_ref[...] = v
    vt_ref[...] = v.T.astype(vt_ref.dtype)
    o += KV_WIDTH
    zc_ref[...] = proj[:, o:o + zc_ref.shape[1]].astype(zc_ref.dtype)
    o += zc_ref.shape[1]
    zh_ref[...] = proj[:, o:o + zh_ref.shape[1]].astype(zh_ref.dtype)


def _rope_tables(length):
    pos = np.arange(length)
    rows, cols = pos // GRID_W, pos % GRID_W
    half = HEAD_DIM // 2
    inv = ROPE_BASE ** (-np.arange(0, half, 2, dtype=np.float64) / half)
    d = np.arange(HEAD_DIM)
    p = np.where(d[None, :] < half, rows[:, None], cols[:, None]).astype(np.float64)
    ang = p * inv[d % (half // 2)][None, :]
    sign = np.where((d % half) < half // 2, -1.0, 1.0)[None, :]
    cos = np.tile(np.cos(ang), (1, LANES // HEAD_DIM))
    sin = np.tile(np.sin(ang) * sign, (1, LANES // HEAD_DIM))
    return jnp.asarray(cos, F32), jnp.asarray(sin, F32)


def _group_mean_matrix(width, group):
    idx = np.arange(width) // group
    return jnp.asarray((idx[:, None] == idx[None, :]).astype(np.float32) / group, BF16)


def _inproj(x, mod_l, cond_base, norm1, w_in, q_norm, k_norm, seq, rotary, zc_w, zh_w):
    n, d = x.shape
    tm = TOKEN_TILE
    per_seq = max(seq // tm, 1)
    cos, sin = _rope_tables(seq if rotary else tm)
    g = _group_mean_matrix(LANES, HEAD_DIM)
    tab = (lambda i: (i % per_seq, 0)) if rotary else _const
    row = lambda i: (i, 0)
    col = lambda i: (0, i)
    return pl.pallas_call(
        functools.partial(_inproj_kernel, rotary=rotary),
        grid=(n // tm,),
        in_specs=[
            pl.BlockSpec((tm, d), row),
            _mod_spec(d, tm, seq, cond_base),
            pl.BlockSpec((1, d), _const),
            pl.BlockSpec(w_in.shape, _const),
            pl.BlockSpec((1, LANES), _const),
            pl.BlockSpec((1, LANES), _const),
            pl.BlockSpec((LANES, LANES), _const),
            pl.BlockSpec((tm, LANES), tab),
            pl.BlockSpec((tm, LANES), tab),
        ],
        out_specs=[
            pl.BlockSpec((ATTN_WIDTH, tm), col),
            pl.BlockSpec((tm, KV_WIDTH), row),
            pl.BlockSpec((tm, KV_WIDTH), row),
            pl.BlockSpec((tm, KV_WIDTH), row),
            pl.BlockSpec((KV_WIDTH, tm), col),
            pl.BlockSpec((tm, zc_w), row),
            pl.BlockSpec((tm, zh_w), row),
        ],
        out_shape=[
            jax.ShapeDtypeStruct((ATTN_WIDTH, n), BF16),
            jax.ShapeDtypeStruct((n, KV_WIDTH), F32),
            jax.ShapeDtypeStruct((n, KV_WIDTH), BF16),
            jax.ShapeDtypeStruct((n, KV_WIDTH), F32),
            jax.ShapeDtypeStruct((KV_WIDTH, n), BF16),
            jax.ShapeDtypeStruct((n, zc_w), BF16),
            jax.ShapeDtypeStruct((n, zh_w), BF16),
        ],
        compiler_params=_params("parallel"),
        name="inproj",
    )(x, mod_l, norm1.reshape(1, d), w_in,
      jnp.tile(q_norm, LANES // HEAD_DIM).reshape(1, LANES),
      jnp.tile(k_norm, LANES // HEAD_DIM).reshape(1, LANES), g, cos, sin)


def _attn_kernel(*refs, chunks, cache_chunks):
    if cache_chunks:
        qt_ref, k_ref, vt_ref, ck_ref, cvt_ref, o_ref = refs
    else:
        qt_ref, k_ref, vt_ref, o_ref = refs
    h = pl.program_id(1)
    tq = qt_ref.shape[1]
    q4 = jnp.concatenate([qt_ref[g * HEAD_DIM:(g + 1) * HEAD_DIM, :] for g in range(GROUP)], axis=1)
    zero = jnp.zeros_like(q4)
    q_pad = jnp.where(h == 0, jnp.concatenate([q4, zero], axis=0), jnp.concatenate([zero, q4], axis=0))

    def ones_rows(size):
        r = lax.broadcasted_iota(jnp.int32, (BF16_SUBLANES, size), 0)
        return jnp.where(r == 0, 1.0, 0.0).astype(BF16)

    m = jnp.full((1, GROUP * tq), -1e30, F32)
    acc = jnp.zeros((HEAD_DIM + BF16_SUBLANES, GROUP * tq), F32)
    pieces = [(k_ref, vt_ref, c) for c in chunks] + [(ck_ref, cvt_ref, c) for c in cache_chunks]
    for kr, vr, (start, size) in pieces:
        s = _dot(kr[start:start + size, :], q_pad)
        m_new = jnp.maximum(m, jnp.max(s, axis=0, keepdims=True))
        alpha = jnp.exp2(m - m_new)
        p = jnp.exp2(s - m_new).astype(BF16)
        v_aug = jnp.concatenate([vr[:, start:start + size], ones_rows(size)], axis=0)
        acc = acc * alpha + _dot(v_aug, p)
        m = m_new
    out = acc[:HEAD_DIM] / acc[HEAD_DIM:HEAD_DIM + 1]
    rows = jnp.concatenate([out[:, g * tq:(g + 1) * tq] for g in range(GROUP)], axis=0)
    o_ref[...] = rows.T.astype(o_ref.dtype)


def _chunks(n):
    return [(s, min(ATTN_K_CHUNK, n - s)) for s in range(0, n, ATTN_K_CHUNK)]


def _attention(qt, kb, vt, seq, cache=None):
    n = kb.shape[0]
    nb = n // seq
    tq = ATTN_Q_TILE
    nq = seq // tq
    rows = GROUP * HEAD_DIM
    in_specs = [
        pl.BlockSpec((rows, tq), lambda b, h, i: (h, b * nq + i)),
        pl.BlockSpec((seq, KV_WIDTH), lambda b, h, i: (b, 0)),
        pl.BlockSpec((HEAD_DIM, seq), lambda b, h, i: (h, b)),
    ]
    args = [qt, kb, vt]
    cache_chunks = []
    if cache is not None:
        past = cache[0].shape[1]
        in_specs += [
            pl.BlockSpec((None, past, KV_WIDTH), lambda b, h, i: (b, 0, 0)),
            pl.BlockSpec((None, HEAD_DIM, past), lambda b, h, i: (b, h, 0)),
        ]
        args += list(cache)
        cache_chunks = _chunks(past)
    return pl.pallas_call(
        functools.partial(_attn_kernel, chunks=_chunks(seq), cache_chunks=cache_chunks),
        grid=(nb, N_KV_HEADS, nq),
        in_specs=in_specs,
        out_specs=pl.BlockSpec((tq, rows), lambda b, h, i: (b * nq + i, h)),
        out_shape=jax.ShapeDtypeStruct((n, ATTN_WIDTH), BF16),
        compiler_params=_params("parallel", "parallel", "parallel"),
        name="attention",
    )(*args)


def _conformer_kernel(z_ref, zp_ref, zn_ref, dww_ref, dwb_ref, gnw_ref, gnb_ref, g_ref, pww_ref, pwb_ref,
                      o_ref, ubuf_ref, *, tiles_per_seq):
    i = pl.program_id(0)
    tt, c = o_ref.shape
    halo = CONV_HALO
    first = (i % tiles_per_seq) == 0
    last = (i % tiles_per_seq) == tiles_per_seq - 1

    def glu(ref):
        z = ref[...].astype(F32)
        return z[:, :c] * _sigmoid(z[:, c:])

    ubuf_ref[0:halo, :] = jnp.where(first, 0.0, glu(zp_ref))
    ubuf_ref[halo:halo + tt, :] = glu(z_ref)
    ubuf_ref[halo + tt:, :] = jnp.where(last, 0.0, glu(zn_ref))

    pad = (CONV_KSIZE - 1) // 2
    rows = 64
    g = g_ref[...]
    for r in range(tt // rows):
        base = halo + r * rows - pad
        acc = jnp.zeros((rows, c), F32) + dwb_ref[...]
        for k in range(CONV_KSIZE):
            acc = acc + ubuf_ref[base + k:base + k + rows, :] * dww_ref[k:k + 1, :]
        hi, lo = _split(acc)
        mu = _dot(hi, g) + _dot(lo, g)
        dlt = acc - mu
        hi, lo = _split(dlt * dlt)
        var = _dot(hi, g) + _dot(lo, g)
        un = dlt * lax.rsqrt(var + NORM_EPS) * gnw_ref[...] + gnb_ref[...]
        act = un * _sigmoid(un)
        out = _dot(act.astype(BF16), pww_ref[...]) + pwb_ref[...]
        o_ref[r * rows:(r + 1) * rows, :] = out.astype(o_ref.dtype)


def _conformer(zc, dw_w, dw_b, gn_w, gn_b, pw_w, pw_b, seq):
    n, c2 = zc.shape
    c = c2 // 2
    tt = CONV_TILE
    hb = tt // CONV_HALO
    n_halo_blocks = n // CONV_HALO
    dw_w = jnp.concatenate([dw_w, jnp.zeros((1, c), F32)], axis=0)
    g = _group_mean_matrix(c, CONV_GROUP_WIDTH)
    return pl.pallas_call(
        functools.partial(_conformer_kernel, tiles_per_seq=seq // tt),
        grid=(n // tt,),
        in_specs=[
            pl.BlockSpec((tt, c2), lambda i: (i, 0)),
            pl.BlockSpec((CONV_HALO, c2), lambda i: (jnp.maximum(i * hb - 1, 0), 0)),
            pl.BlockSpec((CONV_HALO, c2), lambda i: (jnp.minimum((i + 1) * hb, n_halo_blocks - 1), 0)),
            pl.BlockSpec((CONV_KSIZE + 1, c), _const),
            pl.BlockSpec((1, c), _const),
            pl.BlockSpec((1, c), _const),
            pl.BlockSpec((1, c), _const),
            pl.BlockSpec((c, c), _const),
            pl.BlockSpec((c, c), _const),
            pl.BlockSpec((1, c), _const),
        ],
        out_specs=pl.BlockSpec((tt, c), lambda i: (i, 0)),
        out_shape=jax.ShapeDtypeStruct((n, c), BF16),
        scratch_shapes=[pltpu.VMEM((tt + 2 * CONV_HALO, c), F32)],
        compiler_params=_params("parallel"),
        name="conformer",
    )(zc, zc, zc, dw_w, dw_b.reshape(1, c), gn_w.reshape(1, c), gn_b.reshape(1, c), g,
      pw_w.astype(BF16), pw_b.reshape(1, c))


def _fpad(t):
    return -(-(t + 1) // SUBLANES) * SUBLANES


def _dft_tables(t_blk):
    fpad = _fpad(t_blk)
    t = np.arange(t_blk)
    k = np.arange(fpad)
    ang = ((k[:, None] * t[None, :]) % (2 * t_blk)) * (2.0 * np.pi / (2 * t_blk))
    live = (k <= t_blk)[:, None]
    fwd = np.concatenate([np.where(live, np.cos(ang), 0.0), np.where(live, -np.sin(ang), 0.0)], axis=0)
    ck = np.where((k == 0) | (k == t_blk), 1.0, 2.0)[:, None] / (2 * t_blk)
    inv = np.concatenate([np.where(live, ck * np.cos(ang), 0.0), np.where(live, -ck * np.sin(ang), 0.0)], axis=0).T

    def pair(a):
        a = jnp.asarray(a, F32)
        hi = a.astype(BF16)
        return hi, (a - hi.astype(F32)).astype(BF16)

    return pair(fwd), pair(inv)


def _filter_features(length):
    lag = np.arange(2 * length) - length
    idx = np.minimum(np.abs(lag), length - 1)
    t = np.linspace(0.0, 1.0, length)[idx]
    w_ang = 2.0 * np.pi * idx / length
    bands = np.linspace(1e-4, HYENA_POS_BANDS - 1, HYENA_POS_BANDS)
    z = np.zeros((2 * length, LANES), np.float64)
    z[:, 0] = t
    z[:, 1:1 + HYENA_POS_BANDS] = np.cos(bands[None, :] * w_ang[:, None])
    z[:, 1 + HYENA_POS_BANDS:1 + 2 * HYENA_POS_BANDS] = np.sin(-bands[None, :] * w_ang[:, None])
    z[:, 1 + 2 * HYENA_POS_BANDS] = (lag > -length).astype(np.float64)
    return jnp.asarray(z, F32)


def _hyena_filter_kernel(z_ref, w1_ref, b1_ref, w2_ref, b2_ref, fr_ref, w3_ref, b3_ref, dl_ref,
                         fh_ref, fl_ref, h_ref, prev_ref):
    s = pl.program_id(0)
    fpad = prev_ref.shape[0] // 2
    z = z_ref[...]
    fr = fr_ref[...]

    def dense(a, w_ref_, b_ref_):
        w_hi, w_lo = _split(w_ref_[...])
        return _dot3(a, w_hi, w_lo) + b_ref_[...]

    hid = jnp.sin(fr * dense(z, w1_ref, b1_ref))
    hid = jnp.sin(fr * dense(hid, w2_ref, b2_ref))
    h = dense(hid, w3_ref, b3_ref)
    t = z[:, 0:1]
    live = z[:, 1 + 2 * HYENA_POS_BANDS:2 + 2 * HYENA_POS_BANDS]
    h = h * jnp.exp(-t * dl_ref[...]) * live
    spec = _dot3l(fh_ref[...], fl_ref[...], h)

    @pl.when(s == 0)
    def _():
        prev_ref[...] = jnp.zeros_like(prev_ref)

    row = lax.broadcasted_iota(jnp.int32, (2 * fpad, 1), 0)
    sign = (1 - 2 * ((row % fpad) % 2)).astype(F32)
    h_ref[...] = spec + sign * prev_ref[...]
    prev_ref[...] = spec


def _hyena_filters(length, t_blk, w1, b1, w2, b2, w3, b3, freq, dft_fwd):
    nb = length // t_blk
    fpad = _fpad(t_blk)
    fh = w2.shape[0]
    oc = w3.shape[1] // 2
    c = oc // HYENA_ORDER
    z = _filter_features(length)
    w1p = jnp.concatenate([w1, jnp.zeros((LANES - w1.shape[0], fh), F32)], axis=0)
    w3d = w3.reshape(fh, HYENA_ORDER, 2, c).transpose(2, 0, 1, 3).reshape(2, fh, oc)
    b3d = b3.reshape(HYENA_ORDER, 2, c).transpose(1, 0, 2).reshape(2, 1, oc)
    deltas = np.abs(np.linspace(HYENA_MIN_DECAY, HYENA_MAX_DECAY, c))
    dl = jnp.asarray(np.tile(deltas, HYENA_ORDER)[None, :], F32)
    direction = lambda s: (jnp.where(s >= nb, 0, 1), 0, 0)
    return pl.pallas_call(
        _hyena_filter_kernel,
        grid=(2 * nb,),
        in_specs=[
            pl.BlockSpec((t_blk, LANES), lambda s: (s, 0)),
            pl.BlockSpec((LANES, fh), _const),
            pl.BlockSpec((1, fh), _const),
            pl.BlockSpec((fh, fh), _const),
            pl.BlockSpec((1, fh), _const),
            pl.BlockSpec((1, fh), _const),
            pl.BlockSpec((None, fh, oc), direction),
            pl.BlockSpec((None, 1, oc), direction),
            pl.BlockSpec((1, oc), _const),
            pl.BlockSpec((2 * fpad, t_blk), _const),
            pl.BlockSpec((2 * fpad, t_blk), _const),
        ],
        out_specs=pl.BlockSpec((None, 2 * fpad, oc), lambda s: (s, 0, 0)),
        out_shape=jax.ShapeDtypeStruct((2 * nb, 2 * fpad, oc), F32),
        scratch_shapes=[pltpu.VMEM((2 * fpad, oc), F32)],
        compiler_params=_params("arbitrary"),
        name="hyena_filter",
    )(z, w1p, b1.reshape(1, fh), w2, b2.reshape(1, fh), freq.reshape(1, fh), w3d, b3d, dl,
      dft_fwd[0], dft_fwd[1])


def _hyena_conv_kernel(x_ref, gz_ref, sw_ref, sb_ref, gw_ref, gb_ref, bias_ref, h_ref,
                       fh_ref, fl_ref, gh_ref, gl_ref, o_ref, xs_ref, u_ref, y_ref, *, nb, short_signal, rc):
    length, c = x_ref.shape
    t_blk = length // nb
    fpad = y_ref.shape[0] // 2
    sr = min(HY_SHORT_ROWS, length)

    def short_conv(ref, w_ref_, b_ref_, r0):
        a = ref[pl.ds(r0, sr), :].astype(F32)
        w = BF16_SUBLANES
        before = ref[pl.ds(pl.multiple_of(jnp.maximum(r0 - w, 0), w), w), :].astype(F32)[w - 1:, :]
        after = ref[pl.ds(pl.multiple_of(jnp.minimum(r0 + sr, length - w), w), w), :].astype(F32)[:1, :]
        before = jnp.where(r0 == 0, 0.0, before)
        after = jnp.where(r0 + sr == length, 0.0, after)
        row = lax.broadcasted_iota(jnp.int32, (sr, 1), 0)
        prev = jnp.where(row == 0, before, pltpu.roll(a, 1, axis=0))
        nxt = jnp.where(row == sr - 1, after, pltpu.roll(a, sr - 1, axis=0))
        return prev * w_ref_[0:1, :] + a * w_ref_[1:2, :] + nxt * w_ref_[2:3, :] + b_ref_[...]

    def load_signal(r, _):
        r0 = pl.multiple_of(r * sr, sr)
        if short_signal:
            xs_ref[pl.ds(r0, sr), :] = short_conv(x_ref, sw_ref, sb_ref, r0)
        else:
            xs_ref[pl.ds(r0, sr), :] = x_ref[pl.ds(r0, sr), :].astype(F32)
        return 0

    lax.fori_loop(0, length // sr, load_signal, 0)

    def fwd(j, _):
        xj = xs_ref[pl.ds(pl.multiple_of(j * t_blk, t_blk), t_blk), :]
        u_ref[j] = _dot3l(fh_ref[...], fl_ref[...], xj)
        return 0

    lax.fori_loop(0, nb, fwd, 0)

    def out_block(i, _):
        def mac(r, _):
            re = pl.ds(pl.multiple_of(r * rc, SUBLANES), rc)
            im = pl.ds(pl.multiple_of(fpad + r * rc, SUBLANES), rc)
            are = jnp.zeros((rc, c), F32)
            aim = jnp.zeros((rc, c), F32)
            for j in range(nb):
                d = i - j + nb
                hre, him = h_ref[d, re, :], h_ref[d, im, :]
                ure, uim = u_ref[j, re, :], u_ref[j, im, :]
                are = are + hre * ure - him * uim
                aim = aim + hre * uim + him * ure
            y_ref[re, :] = are
            y_ref[im, :] = aim
            return 0

        lax.fori_loop(0, fpad // rc, mac, 0)
        rows = pl.ds(pl.multiple_of(i * t_blk, t_blk), t_blk)
        conv = _dot3l(gh_ref[...], gl_ref[...], y_ref[...])
        xs_ref[rows, :] = conv + xs_ref[rows, :] * bias_ref[...]
        return 0

    lax.fori_loop(0, nb, out_block, 0)

    def gate(r, _):
        r0 = pl.multiple_of(r * sr, sr)
        o_ref[pl.ds(r0, sr), :] = (xs_ref[pl.ds(r0, sr), :] * short_conv(gz_ref, gw_ref, gb_ref, r0)).astype(o_ref.dtype)
        return 0

    lax.fori_loop(0, length // sr, gate, 0)


def _hyena_conv(x, x_col, zh, gate_col, short_w, short_b, bias, spectra, order, length, t_blk,
                dft_fwd, dft_inv, out_dtype, short_signal):
    c = bias.shape[0]
    nseq = zh.shape[0] // length
    nb = length // t_blk
    fpad = _fpad(t_blk)
    rc = max(r for r in (8, 24, 40) if fpad % r == 0)
    return pl.pallas_call(
        functools.partial(_hyena_conv_kernel, nb=nb, short_signal=short_signal, rc=rc),
        grid=(nseq,),
        in_specs=[
            pl.BlockSpec((length, c), lambda b: (b, x_col), pipeline_mode=_SINGLE),
            pl.BlockSpec((length, c), lambda b: (b, gate_col), pipeline_mode=_SINGLE),
            pl.BlockSpec((3, c), lambda b: (0, x_col)),
            pl.BlockSpec((1, c), lambda b: (0, x_col)),
            pl.BlockSpec((3, c), lambda b: (0, gate_col)),
            pl.BlockSpec((1, c), lambda b: (0, gate_col)),
            pl.BlockSpec((1, c), lambda b: (0, 0)),
            pl.BlockSpec((2 * nb, 2 * fpad, c), lambda b: (0, 0, order), pipeline_mode=_SINGLE),
            pl.BlockSpec((2 * fpad, t_blk), _const, pipeline_mode=_SINGLE),
            pl.BlockSpec((2 * fpad, t_blk), _const, pipeline_mode=_SINGLE),
            pl.BlockSpec((t_blk, 2 * fpad), _const, pipeline_mode=_SINGLE),
            pl.BlockSpec((t_blk, 2 * fpad), _const, pipeline_mode=_SINGLE),
        ],
        out_specs=pl.BlockSpec((length, c), lambda b: (b, 0), pipeline_mode=_SINGLE),
        out_shape=jax.ShapeDtypeStruct((nseq * length, c), out_dtype),
        scratch_shapes=[
            pltpu.VMEM((length, c), F32),
            pltpu.VMEM((nb, 2 * fpad, c), F32),
            pltpu.VMEM((2 * fpad, c), F32),
        ],
        compiler_params=_params("arbitrary"),
        name="hyena_conv",
    )(x, zh, short_w, short_b.reshape(1, -1), short_w, short_b.reshape(1, -1), bias.reshape(1, c), spectra,
      dft_fwd[0], dft_fwd[1], dft_inv[0], dft_inv[1])


def _hyena(zh, short_w, short_b, hy_bias, filt, length):
    t_blk = min(HY_MAX_BLOCK, length)
    dft_fwd, dft_inv = _dft_tables(t_blk)
    spectra = _hyena_filters(length, t_blk, *filt, dft_fwd)
    u1 = _hyena_conv(zh, 0, zh, 1, short_w, short_b, hy_bias[0], spectra, 0, length, t_blk,
                     dft_fwd, dft_inv, F32, True)
    return _hyena_conv(u1, 0, zh, 2, short_w, short_b, hy_bias[1], spectra, 1, length, t_blk,
                       dft_fwd, dft_inv, BF16, False)


def _outproj_kernel(x_ref, a_ref, c_ref, h_ref, mod_ref, wa_ref, wc_ref, wh_ref, o_ref):
    mix = _dot(a_ref[...], wa_ref[...]) + _dot(c_ref[...], wc_ref[...]) + _dot(h_ref[...], wh_ref[...])
    o_ref[...] = x_ref[...] + mod_ref[2:3, :] * mix


def _outproj(x, attn, conv, hy, mod_l, cond_base, w_o, seq):
    n, d = x.shape
    tm = TOKEN_TILE
    wa, wc, wh = w_o[:attn.shape[1]], w_o[attn.shape[1]:attn.shape[1] + conv.shape[1]], w_o[-hy.shape[1]:]
    row = lambda i: (i, 0)
    return pl.pallas_call(
        _outproj_kernel,
        grid=(n // tm,),
        in_specs=[
            pl.BlockSpec((tm, d), row),
            pl.BlockSpec((tm, attn.shape[1]), row),
            pl.BlockSpec((tm, conv.shape[1]), row),
            pl.BlockSpec((tm, hy.shape[1]), row),
            _mod_spec(d, tm, seq, cond_base),
            pl.BlockSpec(wa.shape, _const),
            pl.BlockSpec(wc.shape, _const),
            pl.BlockSpec(wh.shape, _const),
        ],
        out_specs=pl.BlockSpec((tm, d), row),
        out_shape=jax.ShapeDtypeStruct((n, d), F32),
        compiler_params=_params("parallel"),
        name="outproj",
    )(x, attn, conv, hy, mod_l, wa, wc, wh)


def _ffn_kernel(x_ref, xp_ref, xn_ref, mod_ref, n2_ref, wup_ref, dww_ref, dwb_ref, wdn_ref, fn_ref,
                o_ref, pv_ref, pg_ref, acc_ref, *, seq, final):
    i = pl.program_id(0)
    tm, d = x_ref.shape
    ffn = wdn_ref.shape[0]
    nc = ffn // FFN_CHUNKS
    halo = FFN_HALO

    def norm_mod(x):
        ms = jnp.mean(x * x, axis=-1, keepdims=True)
        h = x * lax.rsqrt(ms + NORM_EPS) * n2_ref[...]
        return h * (1.0 + mod_ref[4:5, :]) + mod_ref[3:4, :]

    hext = jnp.concatenate([norm_mod(xp_ref[...]), norm_mod(x_ref[...]), norm_mod(xn_ref[...])], axis=0)
    hb = hext.astype(BF16)

    pos = (i * tm + lax.broadcasted_iota(jnp.int32, (tm, 1), 0)) % seq
    at_start = pos == 0
    at_end = pos == seq - 1

    def dwconv(p_ref, col):
        w = dww_ref[:, col:col + nc]
        prev = jnp.where(at_start, 0.0, p_ref[halo - 1:halo - 1 + tm, :])
        nxt = jnp.where(at_end, 0.0, p_ref[halo + 1:halo + 1 + tm, :])
        return prev * w[0:1] + p_ref[halo:halo + tm, :] * w[1:2] + nxt * w[2:3] + dwb_ref[:, col:col + nc]

    for c in range(FFN_CHUNKS):
        c0 = c * nc
        pv_ref[...] = _dot(hb, wup_ref[:, c0:c0 + nc])
        pg_ref[...] = _dot(hb, wup_ref[:, ffn + c0:ffn + c0 + nc])
        val = dwconv(pv_ref, c0)
        gate = dwconv(pg_ref, ffn + c0)
        act = (gate * _sigmoid(gate) * val).astype(BF16)
        part = _dot(act, wdn_ref[c0:c0 + nc, :])
        if c == 0:
            acc_ref[...] = part
        else:
            acc_ref[...] += part

    out = x_ref[...] + mod_ref[5:6, :] * acc_ref[...]
    if final:
        ms = jnp.mean(out * out, axis=-1, keepdims=True)
        out = out * lax.rsqrt(ms + NORM_EPS) * fn_ref[...]
    o_ref[...] = out


def _ffn(x, mod_l, cond_base, norm2, w_up, dw_w, dw_b, w_down, final_norm, seq, final):
    n, d = x.shape
    tm = TOKEN_TILE
    ffn = w_down.shape[0]
    nc = ffn // FFN_CHUNKS
    hb = tm // FFN_HALO
    n_halo_blocks = n // FFN_HALO
    return pl.pallas_call(
        functools.partial(_ffn_kernel, seq=seq, final=final),
        grid=(n // tm,),
        in_specs=[
            pl.BlockSpec((tm, d), lambda i: (i, 0)),
            pl.BlockSpec((FFN_HALO, d), lambda i: (jnp.maximum(i * hb - 1, 0), 0)),
            pl.BlockSpec((FFN_HALO, d), lambda i: (jnp.minimum((i + 1) * hb, n_halo_blocks - 1), 0)),
            _mod_spec(d, tm, seq, cond_base),
            pl.BlockSpec((1, d), _const),
            pl.BlockSpec(w_up.shape, _const, pipeline_mode=_SINGLE),
            pl.BlockSpec((3, 2 * ffn), _const),
            pl.BlockSpec((1, 2 * ffn), _const),
            pl.BlockSpec(w_down.shape, _const, pipeline_mode=_SINGLE),
            pl.BlockSpec((1, d), _const),
        ],
        out_specs=pl.BlockSpec((tm, d), lambda i: (i, 0)),
        out_shape=jax.ShapeDtypeStruct((n, d), F32),
        scratch_shapes=[
            pltpu.VMEM((tm + 2 * FFN_HALO, nc), F32),
            pltpu.VMEM((tm + 2 * FFN_HALO, nc), F32),
            pltpu.VMEM((tm, d), F32),
        ],
        compiler_params=_params("parallel"),
        name="convffn",
    )(x, x, x, mod_l, norm2.reshape(1, d), w_up, dw_w, dw_b.reshape(1, -1), w_down, final_norm.reshape(1, d))


def kernel(x_prompt, x_sample, cache_k, cache_v, c, c_ctx, norm1, norm2, w_mod, b_mod, w_in, q_norm, k_norm,
           conv_dw_w, conv_dw_b, conv_gn_w, conv_gn_b, conv_pw_w, conv_pw_b, hy_short_w, hy_short_b,
           hy_f_w1, hy_f_b1, hy_f_w2, hy_f_b2, hy_f_w3, hy_f_b3, hy_freq, hy_bias, w_o, ffn_w_up,
           ffn_dw_w, ffn_dw_b, ffn_w_down, final_norm):
    batch, ctx_seq, d = x_prompt.shape
    dec_batch, lat_seq, _ = x_sample.shape
    past = cache_k.shape[2]
    depth = norm1.shape[0]
    conv_w = conv_pw_w.shape[1]
    hy_w = hy_bias.shape[2]
    assert 1 + dec_batch <= SUBLANES

    cond = jnp.concatenate([c_ctx[None, :], c, jnp.zeros((SUBLANES - 1 - dec_batch, d), F32)], axis=0)
    mod = _modulation(cond, w_mod, b_mod).transpose(0, 2, 1, 3)

    xs = [x_prompt.reshape(batch * ctx_seq, d), x_sample.reshape(dec_batch * lat_seq, d)]
    seqs = [ctx_seq, lat_seq]
    cond_bases = [0, 1]
    new_k, new_v = [], []
    for l in range(depth):
        w_in_l = w_in[l].astype(BF16)
        w_o_l = w_o[l].astype(BF16)
        w_up_l = ffn_w_up[l].astype(BF16)
        w_down_l = ffn_w_down[l].astype(BF16)
        filt = (hy_f_w1[l], hy_f_b1[l], hy_f_w2[l], hy_f_b2[l], hy_f_w3[l], hy_f_b3[l], hy_freq[l])
        for path in range(2):
            x, seq, base = xs[path], seqs[path], cond_bases[path]
            qt, k, kb, v, vt, zc, zh = _inproj(x, mod[l], base, norm1[l], w_in_l, q_norm[l], k_norm[l],
                                               seq, path == 1, 2 * conv_w, (HYENA_ORDER + 1) * hy_w)
            if path == 0:
                new_k.append(k.reshape(batch, ctx_seq, N_KV_HEADS, HEAD_DIM))
                new_v.append(v.reshape(batch, ctx_seq, N_KV_HEADS, HEAD_DIM))
                cache = None
            else:
                cache = (cache_k[:, l].reshape(dec_batch, past, KV_WIDTH).astype(BF16),
                         cache_v[:, l].reshape(dec_batch, past, KV_WIDTH).transpose(0, 2, 1).astype(BF16))
            attn = _attention(qt, kb, vt, seq, cache)
            conv = _conformer(zc, conv_dw_w[l], conv_dw_b[l], conv_gn_w[l], conv_gn_b[l], conv_pw_w[l],
                              conv_pw_b[l], seq)
            hy = _hyena(zh, hy_short_w[l], hy_short_b[l], hy_bias[l], filt, seq)
            x = _outproj(x, attn, conv, hy, mod[l], base, w_o_l, seq)
            xs[path] = _ffn(x, mod[l], base, norm2[l], w_up_l, ffn_dw_w[l], ffn_dw_b[l], w_down_l,
                            final_norm, seq, l == depth - 1)

    y_prompt = xs[0].reshape(batch, ctx_seq, d)
    y_sample = xs[1].reshape(dec_batch, lat_seq, d)
    return y_prompt, y_sample, jnp.stack(new_k, axis=1), jnp.stack(new_v, axis=1)
```

```python
import functools
import math

import numpy as np
import jax
import jax.numpy as jnp
from jax import lax
from jax.experimental import pallas as pl
from jax.experimental.pallas import tpu as pltpu

F32 = jnp.float32
BF16 = jnp.bfloat16

GRID_W = 64
N_HEADS = 8
N_KV_HEADS = 2
HEAD_DIM = 64
GROUP = N_HEADS // N_KV_HEADS
ATTN_WIDTH = N_HEADS * HEAD_DIM
KV_WIDTH = N_KV_HEADS * HEAD_DIM
CONV_GROUP_WIDTH = 64
CONV_KSIZE = 31
HYENA_ORDER = 2
HYENA_POS_BANDS = 16
HYENA_MIN_DECAY = math.log(1e-2) / 1.5
HYENA_MAX_DECAY = math.log(1e-2) / 0.3
ROPE_BASE = 10000.0
NORM_EPS = 1e-6

LANES = 128
SUBLANES = 8
BF16_SUBLANES = 16
VMEM_LIMIT = 56 * 1024 * 1024

TOKEN_TILE = 512
CONV_TILE = 256
CONV_HALO = 16
ATTN_Q_TILE = 512
ATTN_K_CHUNK = 512
HY_MAX_BLOCK = 512
HY_SHORT_ROWS = 512
HY_STEP_ROWS = 2048
HY_SINGLE_BUFFER_BYTES = 8 * 1024 * 1024
FFN_HALO = 8
FFN_CHUNKS = 2

_SINGLE = pl.Buffered(1)


def _params(*sem):
    return pltpu.CompilerParams(dimension_semantics=sem, vmem_limit_bytes=VMEM_LIMIT)


def _split(x):
    hi = x.astype(BF16)
    lo = (x - hi.astype(F32)).astype(BF16)
    return hi, lo


def _dot(a, b):
    return jnp.dot(a, b, preferred_element_type=F32)


def _dot3(a, b_hi, b_lo):
    a_hi, a_lo = _split(a)
    return _dot(a_hi, b_hi) + _dot(a_hi, b_lo) + _dot(a_lo, b_hi)


def _dot3l(a_hi, a_lo, b):
    b_hi, b_lo = _split(b)
    return _dot(a_hi, b_hi) + _dot(a_lo, b_hi) + _dot(a_hi, b_lo)


def _sigmoid(x):
    return 1.0 / (1.0 + jnp.exp(-x))


def _const(*_):
    return (0, 0)


def _mod_kernel(cond_ref, w_ref, b_ref, o_ref):
    c = cond_ref[...]
    s = c * _sigmoid(c)
    o_ref[...] = _dot(s.astype(BF16), w_ref[...].astype(BF16)) + b_ref[...]


def _modulation(cond, w_mod, b_mod):
    depth, d, _ = w_mod.shape
    return pl.pallas_call(
        _mod_kernel,
        grid=(depth, 6),
        in_specs=[
            pl.BlockSpec((SUBLANES, d), lambda l, j: (0, 0)),
            pl.BlockSpec((None, d, d), lambda l, j: (l, 0, j)),
            pl.BlockSpec((None, 1, d), lambda l, j: (l, 0, j)),
        ],
        out_specs=pl.BlockSpec((None, None, SUBLANES, d), lambda l, j: (l, j, 0, 0)),
        out_shape=jax.ShapeDtypeStruct((depth, 6, SUBLANES, d), F32),
        compiler_params=_params("arbitrary", "arbitrary"),
        name="modulation",
    )(cond, w_mod, b_mod.reshape(depth, 1, 6 * d))


def _mod_spec(d, tile, seq, cond_base):
    per_seq = seq // tile if seq >= tile else 0
    if cond_base == 0:
        return pl.BlockSpec((None, 6, d), lambda i: (0, 0, 0))
    return pl.BlockSpec((None, 6, d), lambda i: (cond_base + i // per_seq, 0, 0))


def _inproj_kernel(x_ref, mod_ref, n1_ref, w_ref, qn_ref, kn_ref, g_ref, cos_ref, sin_ref,
                   qt_ref, k_ref, kb_ref, v_ref, vt_ref, zc_ref, zh_ref, *, rotary):
    x = x_ref[...]
    ms = jnp.mean(x * x, axis=-1, keepdims=True)
    h = x * lax.rsqrt(ms + NORM_EPS) * n1_ref[...]
    h = h * (1.0 + mod_ref[1:2, :]) + mod_ref[0:1, :]
    proj = _dot(h.astype(BF16), w_ref[...])

    g = g_ref[...]
    lane = lax.broadcasted_iota(jnp.int32, (1, LANES), 1)
    low_half = (lane % 32) < 16

    def head_norm(t, w):
        hi, lo = _split(t * t)
        msq = _dot(hi, g) + _dot(lo, g)
        t = t * lax.rsqrt(msq + NORM_EPS) * w
        if rotary:
            partner = jnp.where(low_half, pltpu.roll(t, LANES - 16, axis=1), pltpu.roll(t, 16, axis=1))
            t = t * cos_ref[...] + partner * sin_ref[...]
        return t

    scale = HEAD_DIM ** -0.5 * math.log2(math.e)
    for c in range(ATTN_WIDTH // LANES):
        sl = slice(c * LANES, (c + 1) * LANES)
        qn = head_norm(proj[:, sl], qn_ref[...]) * scale
        qt_ref[sl, :] = qn.T.astype(qt_ref.dtype)

    kn = head_norm(proj[:, ATTN_WIDTH:ATTN_WIDTH + KV_WIDTH], kn_ref[...])
    k_ref[...] = kn
    kb_ref[...] = kn.astype(kb_ref.dtype)
    o = ATTN_WIDTH + KV_WIDTH
    v = proj[:, o:o + KV_WIDTH]
    v_ref[...] = v
    vt_ref[...] = v.T.astype(vt_ref.dtype)
    o += KV_WIDTH
    zc_ref[...] = proj[:, o:o + zc_ref.shape[1]].astype(zc_ref.dtype)
    o += zc_ref.shape[1]
    zh_ref[...] = proj[:, o:o + zh_ref.shape[1]].astype(zh_ref.dtype)


def _rope_tables(length):
    pos = np.arange(length)
    rows, cols = pos // GRID_W, pos % GRID_W
    half = HEAD_DIM // 2
    inv = ROPE_BASE ** (-np.arange(0, half, 2, dtype=np.float64) / half)
    d = np.arange(HEAD_DIM)
    p = np.where(d[None, :] < half, rows[:, None], cols[:, None]).astype(np.float64)
    ang = p * inv[d % (half // 2)][None, :]
    sign = np.where((d % half) < half // 2, -1.0, 1.0)[None, :]
    cos = np.tile(np.cos(ang), (1, LANES // HEAD_DIM))
    sin = np.tile(np.sin(ang) * sign, (1, LANES // HEAD_DIM))
    return jnp.asarray(cos, F32), jnp.asarray(sin, F32)


def _group_mean_matrix(width, group):
    idx = np.arange(width) // group
    return jnp.asarray((idx[:, None] == idx[None, :]).astype(np.float32) / group, BF16)


def _inproj(x, mod_l, cond_base, norm1, w_in, q_norm, k_norm, seq, rotary, zc_w, zh_w):
    n, d = x.shape
    tm = TOKEN_TILE
    per_seq = max(seq // tm, 1)
    cos, sin = _rope_tables(seq if rotary else tm)
    g = _group_mean_matrix(LANES, HEAD_DIM)
    tab = (lambda i: (i % per_seq, 0)) if rotary else _const
    row = lambda i: (i, 0)
    col = lambda i: (0, i)
    return pl.pallas_call(
        functools.partial(_inproj_kernel, rotary=rotary),
        grid=(n // tm,),
        in_specs=[
            pl.BlockSpec((tm, d), row),
            _mod_spec(d, tm, seq, cond_base),
            pl.BlockSpec((1, d), _const),
            pl.BlockSpec(w_in.shape, _const),
            pl.BlockSpec((1, LANES), _const),
            pl.BlockSpec((1, LANES), _const),
            pl.BlockSpec((LANES, LANES), _const),
            pl.BlockSpec((tm, LANES), tab),
            pl.BlockSpec((tm, LANES), tab),
        ],
        out_specs=[
            pl.BlockSpec((ATTN_WIDTH, tm), col),
            pl.BlockSpec((tm, KV_WIDTH), row),
            pl.BlockSpec((tm, KV_WIDTH), row),
            pl.BlockSpec((tm, KV_WIDTH), row),
            pl.BlockSpec((KV_WIDTH, tm), col),
            pl.BlockSpec((tm, zc_w), row),
            pl.BlockSpec((tm, zh_w), row),
        ],
        out_shape=[
            jax.ShapeDtypeStruct((ATTN_WIDTH, n), BF16),
            jax.ShapeDtypeStruct((n, KV_WIDTH), F32),
            jax.ShapeDtypeStruct((n, KV_WIDTH), BF16),
            jax.ShapeDtypeStruct((n, KV_WIDTH), F32),
            jax.ShapeDtypeStruct((KV_WIDTH, n), BF16),
            jax.ShapeDtypeStruct((n, zc_w), BF16),
            jax.ShapeDtypeStruct((n, zh_w), BF16),
        ],
        compiler_params=_params("parallel"),
        name="inproj",
    )(x, mod_l, norm1.reshape(1, d), w_in,
      jnp.tile(q_norm, LANES // HEAD_DIM).reshape(1, LANES),
      jnp.tile(k_norm, LANES // HEAD_DIM).reshape(1, LANES), g, cos, sin)


def _attn_kernel(*refs, chunks, cache_chunks):
    if cache_chunks:
        qt_ref, k_ref, vt_ref, ck_ref, cvt_ref, o_ref = refs
    else:
        qt_ref, k_ref, vt_ref, o_ref = refs
    h = pl.program_id(1)
    tq = qt_ref.shape[1]
    q4 = jnp.concatenate([qt_ref[g * HEAD_DIM:(g + 1) * HEAD_DIM, :] for g in range(GROUP)], axis=1)
    zero = jnp.zeros_like(q4)
    q_pad = jnp.where(h == 0, jnp.concatenate([q4, zero], axis=0), jnp.concatenate([zero, q4], axis=0))

    def ones_rows(size):
        r = lax.broadcasted_iota(jnp.int32, (BF16_SUBLANES, size), 0)
        return jnp.where(r == 0, 1.0, 0.0).astype(BF16)

    m = jnp.full((1, GROUP * tq), -1e30, F32)
    acc = jnp.zeros((HEAD_DIM + BF16_SUBLANES, GROUP * tq), F32)
    pieces = [(k_ref, vt_ref, c) for c in chunks] + [(ck_ref, cvt_ref, c) for c in cache_chunks]
    for kr, vr, (start, size) in pieces:
        s = _dot(kr[start:start + size, :], q_pad)
        m_new = jnp.maximum(m, jnp.max(s, axis=0, keepdims=True))
        alpha = jnp.exp2(m - m_new)
        p = jnp.exp2(s - m_new).astype(BF16)
        v_aug = jnp.concatenate([vr[:, start:start + size], ones_rows(size)], axis=0)
        acc = acc * alpha + _dot(v_aug, p)
        m = m_new
    out = acc[:HEAD_DIM] / acc[HEAD_DIM:HEAD_DIM + 1]
    rows = jnp.concatenate([out[:, g * tq:(g + 1) * tq] for g in range(GROUP)], axis=0)
    o_ref[...] = rows.T.astype(o_ref.dtype)


def _chunks(n):
    return [(s, min(ATTN_K_CHUNK, n - s)) for s in range(0, n, ATTN_K_CHUNK)]


def _attention(qt, kb, vt, seq, cache=None):
    n = kb.shape[0]
    nb = n // seq
    tq = min(ATTN_Q_TILE, seq)
    nq = seq // tq
    rows = GROUP * HEAD_DIM
    in_specs = [
        pl.BlockSpec((rows, tq), lambda b, h, i: (h, b * nq + i)),
        pl.BlockSpec((seq, KV_WIDTH), lambda b, h, i: (b, 0)),
        pl.BlockSpec((HEAD_DIM, seq), lambda b, h, i: (h, b)),
    ]
    args = [qt, kb, vt]
    cache_chunks = []
    if cache is not None:
        past = cache[0].shape[1]
        in_specs += [
            pl.BlockSpec((None, past, KV_WIDTH), lambda b, h, i: (b, 0, 0)),
            pl.BlockSpec((None, HEAD_DIM, past), lambda b, h, i: (b, h, 0)),
        ]
        args += list(cache)
        cache_chunks = _chunks(past)
    return pl.pallas_call(
        functools.partial(_attn_kernel, chunks=_chunks(seq), cache_chunks=cache_chunks),
        grid=(nb, N_KV_HEADS, nq),
        in_specs=in_specs,
        out_specs=pl.BlockSpec((tq, rows), lambda b, h, i: (b * nq + i, h)),
        out_shape=jax.ShapeDtypeStruct((n, ATTN_WIDTH), BF16),
        compiler_params=_params("parallel", "parallel", "parallel"),
        name="attention",
    )(*args)


def _conformer_kernel(z_ref, zp_ref, zn_ref, dww_ref, dwb_ref, gnw_ref, gnb_ref, g_ref, pww_ref, pwb_ref,
                      o_ref, ubuf_ref, cbuf_ref, *, tiles_per_seq):
    i = pl.program_id(0)
    tt, c = o_ref.shape
    halo = CONV_HALO
    first = (i % tiles_per_seq) == 0
    last = (i % tiles_per_seq) == tiles_per_seq - 1

    def glu(ref):
        z = ref[...].astype(F32)
        return z[:, :c] * _sigmoid(z[:, c:])

    ubuf_ref[0, 0:halo, :] = jnp.where(first, 0.0, glu(zp_ref))
    ubuf_ref[0, halo:halo + tt, :] = glu(z_ref)
    ubuf_ref[0, halo + tt:, :] = jnp.where(last, 0.0, glu(zn_ref))
    span = tt + 2 * halo - SUBLANES
    for s in range(1, SUBLANES):
        ubuf_ref[s, 0:span, :] = ubuf_ref[0, s:s + span, :]

    pad = (CONV_KSIZE - 1) // 2
    rows = 64
    for r in range(tt // rows):
        acc = jnp.zeros((rows, c), F32) + dwb_ref[...]
        for k in range(CONV_KSIZE):
            off = halo + r * rows - pad + k
            s = off % SUBLANES
            acc = acc + ubuf_ref[s, off - s:off - s + rows, :] * dww_ref[k:k + 1, :]
        cbuf_ref[r * rows:(r + 1) * rows, :] = acc

    g = g_ref[...]
    u = cbuf_ref[...]
    hi, lo = _split(u)
    mu = _dot(hi, g) + _dot(lo, g)
    dlt = u - mu
    hi, lo = _split(dlt * dlt)
    var = _dot(hi, g) + _dot(lo, g)
    un = dlt * lax.rsqrt(var + NORM_EPS) * gnw_ref[...] + gnb_ref[...]
    act = un * _sigmoid(un)
    o_ref[...] = (_dot(act.astype(BF16), pww_ref[...]) + pwb_ref[...]).astype(o_ref.dtype)


def _conformer(zc, dw_w, dw_b, gn_w, gn_b, pw_w, pw_b, seq):
    n, c2 = zc.shape
    c = c2 // 2
    tt = CONV_TILE
    hb = tt // CONV_HALO
    n_halo_blocks = n // CONV_HALO
    dw_w = jnp.concatenate([dw_w, jnp.zeros((1, c), F32)], axis=0)
    g = _group_mean_matrix(c, CONV_GROUP_WIDTH)
    return pl.pallas_call(
        functools.partial(_conformer_kernel, tiles_per_seq=seq // tt),
        grid=(n // tt,),
        in_specs=[
            pl.BlockSpec((tt, c2), lambda i: (i, 0)),
            pl.BlockSpec((CONV_HALO, c2), lambda i: (jnp.maximum(i * hb - 1, 0), 0)),
            pl.BlockSpec((CONV_HALO, c2), lambda i: (jnp.minimum((i + 1) * hb, n_halo_blocks - 1), 0)),
            pl.BlockSpec((CONV_KSIZE + 1, c), _const),
            pl.BlockSpec((1, c), _const),
            pl.BlockSpec((1, c), _const),
            pl.BlockSpec((1, c), _const),
            pl.BlockSpec((c, c), _const),
            pl.BlockSpec((c, c), _const),
            pl.BlockSpec((1, c), _const),
        ],
        out_specs=pl.BlockSpec((tt, c), lambda i: (i, 0)),
        out_shape=jax.ShapeDtypeStruct((n, c), BF16),
        scratch_shapes=[pltpu.VMEM((SUBLANES, tt + 2 * CONV_HALO, c), F32), pltpu.VMEM((tt, c), F32)],
        compiler_params=_params("parallel"),
        name="conformer",
    )(zc, zc, zc, dw_w, dw_b.reshape(1, c), gn_w.reshape(1, c), gn_b.reshape(1, c), g,
      pw_w.astype(BF16), pw_b.reshape(1, c))


def _fpad(t):
    return -(-(t + 1) // SUBLANES) * SUBLANES


def _dft_tables(t_blk):
    fpad = _fpad(t_blk)
    t = np.arange(t_blk)
    k = np.arange(fpad)
    ang = ((k[:, None] * t[None, :]) % (2 * t_blk)) * (2.0 * np.pi / (2 * t_blk))
    live = (k <= t_blk)[:, None]
    fwd = np.concatenate([np.where(live, np.cos(ang), 0.0), np.where(live, -np.sin(ang), 0.0)], axis=0)
    ck = np.where((k == 0) | (k == t_blk), 1.0, 2.0)[:, None] / (2 * t_blk)
    inv = np.concatenate([np.where(live, ck * np.cos(ang), 0.0), np.where(live, -ck * np.sin(ang), 0.0)], axis=0).T

    return jnp.asarray(fwd, BF16), jnp.asarray(inv, BF16)


def _filter_features(length):
    lag = np.arange(2 * length) - length
    idx = np.minimum(np.abs(lag), length - 1)
    t = np.linspace(0.0, 1.0, length)[idx]
    w_ang = 2.0 * np.pi * idx / length
    bands = np.linspace(1e-4, HYENA_POS_BANDS - 1, HYENA_POS_BANDS)
    z = np.zeros((2 * length, LANES), np.float64)
    z[:, 0] = t
    z[:, 1:1 + HYENA_POS_BANDS] = np.cos(bands[None, :] * w_ang[:, None])
    z[:, 1 + HYENA_POS_BANDS:1 + 2 * HYENA_POS_BANDS] = np.sin(-bands[None, :] * w_ang[:, None])
    z[:, 1 + 2 * HYENA_POS_BANDS] = (lag > -length).astype(np.float64)
    return jnp.asarray(z, F32), jnp.asarray(z.T, F32)


def _hyena_filter_kernel(z_ref, zt_ref, w1t_ref, b1_ref, w2t_ref, b2_ref, fr_ref, w3_ref, b3_ref, dl_ref,
                         fh_ref, h_ref, prev_ref):
    s = pl.program_id(0)
    fpad = prev_ref.shape[0] // 2
    z = z_ref[...]
    fr = fr_ref[...]

    def dense_t(wt_ref, a, b_ref_):
        w_hi, w_lo = _split(wt_ref[...])
        return _dot3l(w_hi, w_lo, a) + b_ref_[...]

    hid = jnp.sin(fr * dense_t(w1t_ref, zt_ref[...], b1_ref))
    hid = jnp.sin(fr * dense_t(w2t_ref, hid, b2_ref)).T
    w_hi, w_lo = _split(w3_ref[...])
    h = _dot3(hid, w_hi, w_lo) + b3_ref[...]
    t = z[:, 0:1]
    live = z[:, 1 + 2 * HYENA_POS_BANDS:2 + 2 * HYENA_POS_BANDS]
    h = h * jnp.exp(-t * dl_ref[...]) * live
    spec = _dot(fh_ref[...], h.astype(BF16))

    @pl.when(s == 0)
    def _():
        prev_ref[...] = jnp.zeros_like(prev_ref)

    row = lax.broadcasted_iota(jnp.int32, (2 * fpad, 1), 0)
    sign = (1 - 2 * ((row % fpad) % 2)).astype(F32)
    h_ref[...] = spec + sign * prev_ref[...]
    prev_ref[...] = spec


def _hyena_filters(length, t_blk, w1, b1, w2, b2, w3, b3, freq, dft_fwd):
    nb = length // t_blk
    fpad = _fpad(t_blk)
    fh = w2.shape[0]
    oc = w3.shape[1] // 2
    c = oc // HYENA_ORDER
    z, zt = _filter_features(length)
    w1t = jnp.concatenate([w1, jnp.zeros((LANES - w1.shape[0], fh), F32)], axis=0).T
    w3d = w3.reshape(fh, HYENA_ORDER, 2, c).transpose(2, 0, 1, 3).reshape(2, fh, oc)
    b3d = b3.reshape(HYENA_ORDER, 2, c).transpose(1, 0, 2).reshape(2, 1, oc)
    deltas = np.abs(np.linspace(HYENA_MIN_DECAY, HYENA_MAX_DECAY, c))
    dl = jnp.asarray(np.tile(deltas, HYENA_ORDER)[None, :], F32)
    direction = lambda s: (jnp.where(s >= nb, 0, 1), 0, 0)
    return pl.pallas_call(
        _hyena_filter_kernel,
        grid=(2 * nb,),
        in_specs=[
            pl.BlockSpec((t_blk, LANES), lambda s: (s, 0)),
            pl.BlockSpec((LANES, t_blk), lambda s: (0, s)),
            pl.BlockSpec((fh, LANES), _const),
            pl.BlockSpec((fh, 1), _const),
            pl.BlockSpec((fh, fh), _const),
            pl.BlockSpec((fh, 1), _const),
            pl.BlockSpec((fh, 1), _const),
            pl.BlockSpec((None, fh, oc), direction),
            pl.BlockSpec((None, 1, oc), direction),
            pl.BlockSpec((1, oc), _const),
            pl.BlockSpec((2 * fpad, t_blk), _const),
        ],
        out_specs=pl.BlockSpec((None, 2 * fpad, oc), lambda s: (s, 0, 0)),
        out_shape=jax.ShapeDtypeStruct((2 * nb, 2 * fpad, oc), F32),
        scratch_shapes=[pltpu.VMEM((2 * fpad, oc), F32)],
        compiler_params=_params("arbitrary"),
        name="hyena_filter",
    )(z, zt, w1t, b1.reshape(fh, 1), w2.T, b2.reshape(fh, 1), freq.reshape(fh, 1), w3d, b3d, dl,
      dft_fwd)


def _hyena_conv_kernel(x_ref, gz_ref, sw_ref, sb_ref, gw_ref, gb_ref, bias_ref, h_ref, f_ref, g_ref,
                       o_ref, xs_ref, u_ref, y_ref, *, length, nb, short_signal, rc):
    rows, c = x_ref.shape
    t_blk = length // nb
    n_blocks = rows // t_blk
    fpad = y_ref.shape[0] // 2
    sr = min(HY_SHORT_ROWS, rows)

    def short_conv(ref, w_ref_, b_ref_, r0):
        a = ref[pl.ds(r0, sr), :].astype(F32)
        w = BF16_SUBLANES
        before = ref[pl.ds(pl.multiple_of(jnp.maximum(r0 - w, 0), w), w), :].astype(F32)[w - 1:, :]
        after = ref[pl.ds(pl.multiple_of(jnp.minimum(r0 + sr, rows - w), w), w), :].astype(F32)[:1, :]
        row = lax.broadcasted_iota(jnp.int32, (sr, 1), 0)
        pos = (r0 + row) % length
        prev = jnp.where(row == 0, before, pltpu.roll(a, 1, axis=0))
        prev = jnp.where(pos == 0, 0.0, prev)
        nxt = jnp.where(row == sr - 1, after, pltpu.roll(a, sr - 1, axis=0))
        nxt = jnp.where(pos == length - 1, 0.0, nxt)
        return prev * w_ref_[0:1, :] + a * w_ref_[1:2, :] + nxt * w_ref_[2:3, :] + b_ref_[...]

    def load_signal(r, _):
        r0 = pl.multiple_of(r * sr, sr)
        if short_signal:
            xs_ref[pl.ds(r0, sr), :] = short_conv(x_ref, sw_ref, sb_ref, r0)
        else:
            xs_ref[pl.ds(r0, sr), :] = x_ref[pl.ds(r0, sr), :].astype(F32)
        return 0

    lax.fori_loop(0, rows // sr, load_signal, 0)

    def fwd(j, _):
        xj = xs_ref[pl.ds(pl.multiple_of(j * t_blk, t_blk), t_blk), :]
        u_ref[j] = _dot(f_ref[...], xj.astype(BF16))
        return 0

    lax.fori_loop(0, n_blocks, fwd, 0)

    def out_block(sb, _):
        i = sb % nb
        first = sb - i

        def mac(r, _):
            re = pl.ds(pl.multiple_of(r * rc, SUBLANES), rc)
            im = pl.ds(pl.multiple_of(fpad + r * rc, SUBLANES), rc)
            are = jnp.zeros((rc, c), F32)
            aim = jnp.zeros((rc, c), F32)
            for j in range(nb):
                d = i - j + nb
                hre, him = h_ref[d, re, :], h_ref[d, im, :]
                ure, uim = u_ref[first + j, re, :], u_ref[first + j, im, :]
                are = are + hre * ure - him * uim
                aim = aim + hre * uim + him * ure
            y_ref[re, :] = are
            y_ref[im, :] = aim
            return 0

        lax.fori_loop(0, fpad // rc, mac, 0)
        blk = pl.ds(pl.multiple_of(sb * t_blk, t_blk), t_blk)
        conv = _dot(g_ref[...], y_ref[...].astype(BF16))
        xs_ref[blk, :] = conv + xs_ref[blk, :] * bias_ref[...]
        return 0

    lax.fori_loop(0, n_blocks, out_block, 0)

    def gate(r, _):
        r0 = pl.multiple_of(r * sr, sr)
        o_ref[pl.ds(r0, sr), :] = (xs_ref[pl.ds(r0, sr), :] * short_conv(gz_ref, gw_ref, gb_ref, r0)).astype(o_ref.dtype)
        return 0

    lax.fori_loop(0, rows // sr, gate, 0)


def _hyena_conv(x, x_col, zh, gate_col, short_w, short_b, bias, spectra, order, length, t_blk,
                dft_fwd, dft_inv, short_signal):
    c = bias.shape[0]
    n = zh.shape[0]
    nb = length // t_blk
    fpad = _fpad(t_blk)
    rc = max(r for r in (8, 24, 40) if fpad % r == 0)
    rows = min(max(length, HY_STEP_ROWS), n)
    big = _SINGLE if 2 * nb * 2 * fpad * c * 4 > HY_SINGLE_BUFFER_BYTES else None
    return pl.pallas_call(
        functools.partial(_hyena_conv_kernel, length=length, nb=nb, short_signal=short_signal, rc=rc),
        grid=(n // rows,),
        in_specs=[
            pl.BlockSpec((rows, c), lambda b: (b, x_col)),
            pl.BlockSpec((rows, c), lambda b: (b, gate_col)),
            pl.BlockSpec((3, c), lambda b: (0, x_col)),
            pl.BlockSpec((1, c), lambda b: (0, x_col)),
            pl.BlockSpec((3, c), lambda b: (0, gate_col)),
            pl.BlockSpec((1, c), lambda b: (0, gate_col)),
            pl.BlockSpec((1, c), lambda b: (0, 0)),
            pl.BlockSpec((2 * nb, 2 * fpad, c), lambda b: (0, 0, order), pipeline_mode=big),
            pl.BlockSpec((2 * fpad, t_blk), _const),
            pl.BlockSpec((t_blk, 2 * fpad), _const),
        ],
        out_specs=pl.BlockSpec((rows, c), lambda b: (b, 0)),
        out_shape=jax.ShapeDtypeStruct((n, c), BF16),
        scratch_shapes=[
            pltpu.VMEM((rows, c), F32),
            pltpu.VMEM((rows // t_blk, 2 * fpad, c), F32),
            pltpu.VMEM((2 * fpad, c), F32),
        ],
        compiler_params=_params("arbitrary"),
        name="hyena_conv",
    )(x, zh, short_w, short_b.reshape(1, -1), short_w, short_b.reshape(1, -1), bias.reshape(1, c), spectra,
      dft_fwd, dft_inv)


def _hyena(zh, short_w, short_b, hy_bias, filt, length):
    t_blk = min(HY_MAX_BLOCK, length)
    dft_fwd, dft_inv = _dft_tables(t_blk)
    spectra = _hyena_filters(length, t_blk, *filt, dft_fwd)
    u1 = _hyena_conv(zh, 0, zh, 1, short_w, short_b, hy_bias[0], spectra, 0, length, t_blk,
                     dft_fwd, dft_inv, True)
    return _hyena_conv(u1, 0, zh, 2, short_w, short_b, hy_bias[1], spectra, 1, length, t_blk,
                       dft_fwd, dft_inv, False)


def _outproj_kernel(x_ref, a_ref, c_ref, h_ref, mod_ref, wa_ref, wc_ref, wh_ref, o_ref):
    mix = _dot(a_ref[...], wa_ref[...]) + _dot(c_ref[...], wc_ref[...]) + _dot(h_ref[...], wh_ref[...])
    o_ref[...] = x_ref[...] + mod_ref[2:3, :] * mix


def _outproj(x, attn, conv, hy, mod_l, cond_base, w_o, seq):
    n, d = x.shape
    tm = TOKEN_TILE
    wa, wc, wh = w_o[:attn.shape[1]], w_o[attn.shape[1]:attn.shape[1] + conv.shape[1]], w_o[-hy.shape[1]:]
    row = lambda i: (i, 0)
    return pl.pallas_call(
        _outproj_kernel,
        grid=(n // tm,),
        in_specs=[
            pl.BlockSpec((tm, d), row),
            pl.BlockSpec((tm, attn.shape[1]), row),
            pl.BlockSpec((tm, conv.shape[1]), row),
            pl.BlockSpec((tm, hy.shape[1]), row),
            _mod_spec(d, tm, seq, cond_base),
            pl.BlockSpec(wa.shape, _const),
            pl.BlockSpec(wc.shape, _const),
            pl.BlockSpec(wh.shape, _const),
        ],
        out_specs=pl.BlockSpec((tm, d), row),
        out_shape=jax.ShapeDtypeStruct((n, d), F32),
        compiler_params=_params("parallel"),
        name="outproj",
    )(x, attn, conv, hy, mod_l, wa, wc, wh)


def _ffn_kernel(x_ref, xp_ref, xn_ref, mod_ref, n2_ref, wup_ref, dww_ref, dwb_ref, wdn_ref, fn_ref,
                o_ref, p_ref, acc_ref, *, seq, final):
    i = pl.program_id(0)
    tm, d = x_ref.shape
    ffn = wdn_ref.shape[0]
    nc = ffn // FFN_CHUNKS
    halo = FFN_HALO
    seg = min(seq, tm)
    nseg = tm // seg

    def norm_mod(x):
        ms = jnp.mean(x * x, axis=-1, keepdims=True)
        h = x * lax.rsqrt(ms + NORM_EPS) * n2_ref[...]
        return h * (1.0 + mod_ref[4:5, :]) + mod_ref[3:4, :]

    zeros = jnp.zeros((halo, d), F32)
    if seq <= tm:
        before, after = zeros, zeros
    else:
        tiles = seq // tm
        before = jnp.where(i % tiles == 0, 0.0, norm_mod(xp_ref[...]))
        after = jnp.where(i % tiles == tiles - 1, 0.0, norm_mod(xn_ref[...]))
    hm = norm_mod(x_ref[...])
    parts = [before]
    for s in range(nseg):
        parts += [hm[s * seg:(s + 1) * seg], after if s == nseg - 1 else zeros]
    hb = jnp.concatenate(parts, axis=0).astype(BF16)

    def dwconv(pr, col):
        w = dww_ref[:, col:col + nc]
        out = []
        for s in range(nseg):
            o = halo + s * (seg + halo)
            out.append(pr[o - 1:o - 1 + seg, :] * w[0:1] + pr[o:o + seg, :] * w[1:2]
                       + pr[o + 1:o + 1 + seg, :] * w[2:3] + dwb_ref[:, col:col + nc])
        return out[0] if nseg == 1 else jnp.concatenate(out, axis=0)

    for c in range(FFN_CHUNKS):
        c0 = c * nc
        p_ref[2 * c] = _dot(hb, wup_ref[:, c0:c0 + nc])
        p_ref[2 * c + 1] = _dot(hb, wup_ref[:, ffn + c0:ffn + c0 + nc])
        val = dwconv(p_ref.at[2 * c], c0)
        gate = dwconv(p_ref.at[2 * c + 1], ffn + c0)
        act = (gate * _sigmoid(gate) * val).astype(BF16)
        part = _dot(act, wdn_ref[c0:c0 + nc, :])
        if c == 0:
            acc_ref[...] = part
        else:
            acc_ref[...] += part

    out = x_ref[...] + mod_ref[5:6, :] * acc_ref[...]
    if final:
        ms = jnp.mean(out * out, axis=-1, keepdims=True)
        out = out * lax.rsqrt(ms + NORM_EPS) * fn_ref[...]
    o_ref[...] = out


def _ffn(x, mod_l, cond_base, norm2, w_up, dw_w, dw_b, w_down, final_norm, seq, final):
    n, d = x.shape
    tm = TOKEN_TILE
    ffn = w_down.shape[0]
    nc = ffn // FFN_CHUNKS
    hb = tm // FFN_HALO
    n_halo_blocks = n // FFN_HALO
    return pl.pallas_call(
        functools.partial(_ffn_kernel, seq=seq, final=final),
        grid=(n // tm,),
        in_specs=[
            pl.BlockSpec((tm, d), lambda i: (i, 0)),
            pl.BlockSpec((FFN_HALO, d), lambda i: (jnp.maximum(i * hb - 1, 0), 0)),
            pl.BlockSpec((FFN_HALO, d), lambda i: (jnp.minimum((i + 1) * hb, n_halo_blocks - 1), 0)),
            _mod_spec(d, tm, seq, cond_base),
            pl.BlockSpec((1, d), _const),
            pl.BlockSpec(w_up.shape, _const, pipeline_mode=_SINGLE),
            pl.BlockSpec((3, 2 * ffn), _const),
            pl.BlockSpec((1, 2 * ffn), _const),
            pl.BlockSpec(w_down.shape, _const, pipeline_mode=_SINGLE),
            pl.BlockSpec((1, d), _const),
        ],
        out_specs=pl.BlockSpec((tm, d), lambda i: (i, 0)),
        out_shape=jax.ShapeDtypeStruct((n, d), F32),
        scratch_shapes=[
            pltpu.VMEM((2 * FFN_CHUNKS, tm + (tm // min(seq, tm) + 1) * FFN_HALO, nc), F32),
            pltpu.VMEM((tm, d), F32),
        ],
        compiler_params=_params("parallel"),
        name="convffn",
    )(x, x, x, mod_l, norm2.reshape(1, d), w_up, dw_w, dw_b.reshape(1, -1), w_down, final_norm.reshape(1, d))


def kernel(x_prompt, x_sample, cache_k, cache_v, c, c_ctx, norm1, norm2, w_mod, b_mod, w_in, q_norm, k_norm,
           conv_dw_w, conv_dw_b, conv_gn_w, conv_gn_b, conv_pw_w, conv_pw_b, hy_short_w, hy_short_b,
           hy_f_w1, hy_f_b1, hy_f_w2, hy_f_b2, hy_f_w3, hy_f_b3, hy_freq, hy_bias, w_o, ffn_w_up,
           ffn_dw_w, ffn_dw_b, ffn_w_down, final_norm):
    batch, ctx_seq, d = x_prompt.shape
    dec_batch, lat_seq, _ = x_sample.shape
    past = cache_k.shape[2]
    depth = norm1.shape[0]
    conv_w = conv_pw_w.shape[1]
    hy_w = hy_bias.shape[2]
    assert 1 + dec_batch <= SUBLANES

    cond = jnp.concatenate([c_ctx[None, :], c, jnp.zeros((SUBLANES - 1 - dec_batch, d), F32)], axis=0)
    mod = _modulation(cond, w_mod, b_mod).transpose(0, 2, 1, 3)

    xs = [x_prompt.reshape(batch * ctx_seq, d), x_sample.reshape(dec_batch * lat_seq, d)]
    seqs = [ctx_seq, lat_seq]
    cond_bases = [0, 1]
    new_k, new_v = [], []
    for l in range(depth):
        w_in_l = w_in[l].astype(BF16)
        w_o_l = w_o[l].astype(BF16)
        w_up_l = ffn_w_up[l].astype(BF16)
        w_down_l = ffn_w_down[l].astype(BF16)
        filt = (hy_f_w1[l], hy_f_b1[l], hy_f_w2[l], hy_f_b2[l], hy_f_w3[l], hy_f_b3[l], hy_freq[l])
        for path in range(2):
            x, seq, base = xs[path], seqs[path], cond_bases[path]
            qt, k, kb, v, vt, zc, zh = _inproj(x, mod[l], base, norm1[l], w_in_l, q_norm[l], k_norm[l],
                                               seq, path == 1, 2 * conv_w, (HYENA_ORDER + 1) * hy_w)
            if path == 0:
                new_k.append(k.reshape(batch, ctx_seq, N_KV_HEADS, HEAD_DIM))
                new_v.append(v.reshape(batch, ctx_seq, N_KV_HEADS, HEAD_DIM))
                cache = None
            else:
                cache = (cache_k[:, l].reshape(dec_batch, past, KV_WIDTH).astype(BF16),
                         cache_v[:, l].reshape(dec_batch, past, KV_WIDTH).transpose(0, 2, 1).astype(BF16))
            attn = _attention(qt, kb, vt, seq, cache)
            conv = _conformer(zc, conv_dw_w[l], conv_dw_b[l], conv_gn_w[l], conv_gn_b[l], conv_pw_w[l],
                              conv_pw_b[l], seq)
            hy = _hyena(zh, hy_short_w[l], hy_short_b[l], hy_bias[l], filt, seq)
            x = _outproj(x, attn, conv, hy, mod[l], base, w_o_l, seq)
            xs[path] = _ffn(x, mod[l], base, norm2[l], w_up_l, ffn_dw_w[l], ffn_dw_b[l], w_down_l,
                            final_norm, seq, l == depth - 1)

    y_prompt = xs[0].reshape(batch, ctx_seq, d)
    y_sample = xs[1].reshape(dec_batch, lat_seq, d)
    return y_prompt, y_sample, jnp.stack(new_k, axis=1), jnp.stack(new_v, axis=1)
```

```python
import functools
import math

import numpy as np
import jax
import jax.numpy as jnp
from jax import lax
from jax.experimental import pallas as pl
from jax.experimental.pallas import tpu as pltpu

F32 = jnp.float32
BF16 = jnp.bfloat16

GRID_W = 64
N_HEADS = 8
N_KV_HEADS = 2
HEAD_DIM = 64
GROUP = N_HEADS // N_KV_HEADS
ATTN_WIDTH = N_HEADS * HEAD_DIM
KV_WIDTH = N_KV_HEADS * HEAD_DIM
CONV_GROUP_WIDTH = 64
CONV_KSIZE = 31
HYENA_ORDER = 2
HYENA_POS_BANDS = 16
HYENA_MIN_DECAY = math.log(1e-2) / 1.5
HYENA_MAX_DECAY = math.log(1e-2) / 0.3
ROPE_BASE = 10000.0
NORM_EPS = 1e-6

LANES = 128
SUBLANES = 8
BF16_SUBLANES = 16
VMEM_LIMIT = 56 * 1024 * 1024

TOKEN_TILE = 512
INPROJ_SLAB = 256
CONV_TILE = 256
CONV_HALO = 16
ATTN_Q_TILE = 1024
ATTN_K_CHUNK = 512
HY_MAX_BLOCK = 512
HY_SHORT_ROWS = 512
HY_STEP_ROWS = 2048
HY_SINGLE_BUFFER_BYTES = 8 * 1024 * 1024
FFN_HALO = 8
FFN_CHUNKS = 1

_SINGLE = pl.Buffered(1)


def _params(*sem):
    return pltpu.CompilerParams(dimension_semantics=sem, vmem_limit_bytes=VMEM_LIMIT)


def _split(x):
    hi = x.astype(BF16)
    lo = (x - hi.astype(F32)).astype(BF16)
    return hi, lo


def _dot(a, b):
    return jnp.dot(a, b, preferred_element_type=F32)


def _dot3(a, b_hi, b_lo):
    a_hi, a_lo = _split(a)
    return _dot(a_hi, b_hi) + _dot(a_hi, b_lo) + _dot(a_lo, b_hi)


def _dot3l(a_hi, a_lo, b):
    b_hi, b_lo = _split(b)
    return _dot(a_hi, b_hi) + _dot(a_lo, b_hi) + _dot(a_hi, b_lo)


def _sigmoid(x):
    return 1.0 / (1.0 + jnp.exp(-x))


def _const(*_):
    return (0, 0)


def _mod_kernel(cond_ref, w_ref, b_ref, o_ref):
    c = cond_ref[...]
    s = c * _sigmoid(c)
    o_ref[...] = _dot(s.astype(BF16), w_ref[...].astype(BF16)) + b_ref[...]


def _modulation(cond, w_mod, b_mod):
    depth, d, _ = w_mod.shape
    return pl.pallas_call(
        _mod_kernel,
        grid=(depth, 6),
        in_specs=[
            pl.BlockSpec((SUBLANES, d), lambda l, j: (0, 0)),
            pl.BlockSpec((None, d, d), lambda l, j: (l, 0, j)),
            pl.BlockSpec((None, 1, d), lambda l, j: (l, 0, j)),
        ],
        out_specs=pl.BlockSpec((None, None, SUBLANES, d), lambda l, j: (l, j, 0, 0)),
        out_shape=jax.ShapeDtypeStruct((depth, 6, SUBLANES, d), F32),
        compiler_params=_params("arbitrary", "arbitrary"),
        name="modulation",
    )(cond, w_mod, b_mod.reshape(depth, 1, 6 * d))


def _mod_spec(d, tile, seq, cond_base):
    per_seq = seq // tile if seq >= tile else 0
    if cond_base == 0:
        return pl.BlockSpec((None, 6, d), lambda i: (0, 0, 0))
    return pl.BlockSpec((None, 6, d), lambda i: (cond_base + i // per_seq, 0, 0))


def _inproj_kernel(x_ref, mod_ref, n1_ref, w_ref, qn_ref, kn_ref, g_ref, cos_ref, sin_ref,
                   qt_ref, k_ref, kb_ref, v_ref, vt_ref, zc_ref, zh_ref, *, rotary):
    x = x_ref[...]
    ms = jnp.mean(x * x, axis=-1, keepdims=True)
    h = x * lax.rsqrt(ms + NORM_EPS) * n1_ref[...]
    h = h * (1.0 + mod_ref[1:2, :]) + mod_ref[0:1, :]
    proj = _dot(h.astype(BF16), w_ref[...])

    def head_norm(t, w_ref_):
        width = t.shape[1]
        msq = _dot((t * t).astype(BF16), g_ref[:width, :width])
        t = t * lax.rsqrt(msq + NORM_EPS) * w_ref_[:, :width]
        if rotary:
            lane = lax.broadcasted_iota(jnp.int32, (1, width), 1)
            partner = jnp.where((lane % 32) < 16, pltpu.roll(t, width - 16, axis=1), pltpu.roll(t, 16, axis=1))
            t = t * cos_ref[:, :width] + partner * sin_ref[:, :width]
        return t

    scale = HEAD_DIM ** -0.5 * math.log2(math.e)
    slab = g_ref.shape[0]
    for c in range(ATTN_WIDTH // slab):
        sl = slice(c * slab, (c + 1) * slab)
        qn = head_norm(proj[:, sl], qn_ref) * scale
        qt_ref[sl, :] = qn.T.astype(qt_ref.dtype)

    kn = head_norm(proj[:, ATTN_WIDTH:ATTN_WIDTH + KV_WIDTH], kn_ref)
    k_ref[...] = kn
    kb_ref[...] = kn.astype(kb_ref.dtype)
    o = ATTN_WIDTH + KV_WIDTH
    v = proj[:, o:o + KV_WIDTH]
    v_ref[...] = v
    vt_ref[...] = v.T.astype(vt_ref.dtype)
    o += KV_WIDTH
    zc_ref[...] = proj[:, o:o + zc_ref.shape[1]].astype(zc_ref.dtype)
    o += zc_ref.shape[1]
    zh_ref[...] = proj[:, o:o + zh_ref.shape[1]].astype(zh_ref.dtype)


def _rope_tables(length, width):
    pos = np.arange(length)
    rows, cols = pos // GRID_W, pos % GRID_W
    half = HEAD_DIM // 2
    inv = ROPE_BASE ** (-np.arange(0, half, 2, dtype=np.float64) / half)
    d = np.arange(HEAD_DIM)
    p = np.where(d[None, :] < half, rows[:, None], cols[:, None]).astype(np.float64)
    ang = p * inv[d % (half // 2)][None, :]
    sign = np.where((d % half) < half // 2, -1.0, 1.0)[None, :]
    cos = np.tile(np.cos(ang), (1, width // HEAD_DIM))
    sin = np.tile(np.sin(ang) * sign, (1, width // HEAD_DIM))
    return jnp.asarray(cos, F32), jnp.asarray(sin, F32)


def _group_mean_matrix(width, group):
    idx = np.arange(width) // group
    return jnp.asarray((idx[:, None] == idx[None, :]).astype(np.float32) / group, BF16)


def _inproj(x, mod_l, cond_base, norm1, w_in, q_norm, k_norm, seq, rotary, zc_w, zh_w):
    n, d = x.shape
    tm = TOKEN_TILE
    per_seq = max(seq // tm, 1)
    slab = INPROJ_SLAB
    cos, sin = _rope_tables(seq if rotary else tm, slab)
    g = _group_mean_matrix(slab, HEAD_DIM)
    tab = (lambda i: (i % per_seq, 0)) if rotary else _const
    row = lambda i: (i, 0)
    col = lambda i: (0, i)
    return pl.pallas_call(
        functools.partial(_inproj_kernel, rotary=rotary),
        grid=(n // tm,),
        in_specs=[
            pl.BlockSpec((tm, d), row),
            _mod_spec(d, tm, seq, cond_base),
            pl.BlockSpec((1, d), _const),
            pl.BlockSpec(w_in.shape, _const),
            pl.BlockSpec((1, slab), _const),
            pl.BlockSpec((1, slab), _const),
            pl.BlockSpec((slab, slab), _const),
            pl.BlockSpec((tm, slab), tab),
            pl.BlockSpec((tm, slab), tab),
        ],
        out_specs=[
            pl.BlockSpec((ATTN_WIDTH, tm), col),
            pl.BlockSpec((tm, KV_WIDTH), row),
            pl.BlockSpec((tm, KV_WIDTH), row),
            pl.BlockSpec((tm, KV_WIDTH), row),
            pl.BlockSpec((KV_WIDTH, tm), col),
            pl.BlockSpec((tm, zc_w), row),
            pl.BlockSpec((tm, zh_w), row),
        ],
        out_shape=[
            jax.ShapeDtypeStruct((ATTN_WIDTH, n), BF16),
            jax.ShapeDtypeStruct((n, KV_WIDTH), F32),
            jax.ShapeDtypeStruct((n, KV_WIDTH), BF16),
            jax.ShapeDtypeStruct((n, KV_WIDTH), F32),
            jax.ShapeDtypeStruct((KV_WIDTH, n), BF16),
            jax.ShapeDtypeStruct((n, zc_w), BF16),
            jax.ShapeDtypeStruct((n, zh_w), BF16),
        ],
        compiler_params=_params("parallel"),
        name="inproj",
    )(x, mod_l, norm1.reshape(1, d), w_in,
      jnp.tile(q_norm, slab // HEAD_DIM).reshape(1, slab),
      jnp.tile(k_norm, slab // HEAD_DIM).reshape(1, slab), g, cos, sin)


def _attn_kernel(*refs, chunks, cache_chunks):
    if cache_chunks:
        qt_ref, k_ref, vt_ref, ck_ref, cvt_ref, o_ref = refs
    else:
        qt_ref, k_ref, vt_ref, o_ref = refs
    h = pl.program_id(1)
    tq = qt_ref.shape[1]
    q4 = jnp.concatenate([qt_ref[g * HEAD_DIM:(g + 1) * HEAD_DIM, :] for g in range(GROUP)], axis=1)
    zero = jnp.zeros_like(q4)
    q_pad = jnp.where(h == 0, jnp.concatenate([q4, zero], axis=0), jnp.concatenate([zero, q4], axis=0))

    def ones_rows(size):
        r = lax.broadcasted_iota(jnp.int32, (BF16_SUBLANES, size), 0)
        return jnp.where(r == 0, 1.0, 0.0).astype(BF16)

    m = jnp.full((1, GROUP * tq), -1e30, F32)
    acc = jnp.zeros((HEAD_DIM + BF16_SUBLANES, GROUP * tq), F32)
    pieces = [(k_ref, vt_ref, c) for c in chunks] + [(ck_ref, cvt_ref, c) for c in cache_chunks]
    for kr, vr, (start, size) in pieces:
        s = _dot(kr[start:start + size, :], q_pad)
        m_new = jnp.maximum(m, jnp.max(s, axis=0, keepdims=True))
        alpha = jnp.exp2(m - m_new)
        p = jnp.exp2(s - m_new).astype(BF16)
        v_aug = jnp.concatenate([vr[:, start:start + size], ones_rows(size)], axis=0)
        acc = acc * alpha + _dot(v_aug, p)
        m = m_new
    out = acc[:HEAD_DIM] / acc[HEAD_DIM:HEAD_DIM + 1]
    rows = jnp.concatenate([out[:, g * tq:(g + 1) * tq] for g in range(GROUP)], axis=0)
    o_ref[...] = rows.T.astype(o_ref.dtype)


def _chunks(n):
    return [(s, min(ATTN_K_CHUNK, n - s)) for s in range(0, n, ATTN_K_CHUNK)]


def _attention(qt, kb, vt, seq, cache=None):
    n = kb.shape[0]
    nb = n // seq
    tq = min(ATTN_Q_TILE, seq)
    nq = seq // tq
    rows = GROUP * HEAD_DIM
    in_specs = [
        pl.BlockSpec((rows, tq), lambda b, h, i: (h, b * nq + i)),
        pl.BlockSpec((seq, KV_WIDTH), lambda b, h, i: (b, 0)),
        pl.BlockSpec((HEAD_DIM, seq), lambda b, h, i: (h, b)),
    ]
    args = [qt, kb, vt]
    cache_chunks = []
    if cache is not None:
        past = cache[0].shape[1]
        in_specs += [
            pl.BlockSpec((None, past, KV_WIDTH), lambda b, h, i: (b, 0, 0)),
            pl.BlockSpec((None, HEAD_DIM, past), lambda b, h, i: (b, h, 0)),
        ]
        args += list(cache)
        cache_chunks = _chunks(past)
    return pl.pallas_call(
        functools.partial(_attn_kernel, chunks=_chunks(seq), cache_chunks=cache_chunks),
        grid=(nb, N_KV_HEADS, nq),
        in_specs=in_specs,
        out_specs=pl.BlockSpec((tq, rows), lambda b, h, i: (b * nq + i, h)),
        out_shape=jax.ShapeDtypeStruct((n, ATTN_WIDTH), BF16),
        compiler_params=_params("parallel", "parallel", "parallel"),
        name="attention",
    )(*args)


def _conformer_kernel(z_ref, zp_ref, zn_ref, dww_ref, dwb_ref, gnw_ref, gnb_ref, g_ref, pww_ref, pwb_ref,
                      o_ref, ubuf_ref, cbuf_ref, *, tiles_per_seq):
    i = pl.program_id(0)
    tt, c = o_ref.shape
    halo = CONV_HALO
    first = (i % tiles_per_seq) == 0
    last = (i % tiles_per_seq) == tiles_per_seq - 1

    def glu(ref):
        z = ref[...].astype(F32)
        return z[:, :c] * _sigmoid(z[:, c:])

    ubuf_ref[0, 0:halo, :] = jnp.where(first, 0.0, glu(zp_ref))
    ubuf_ref[0, halo:halo + tt, :] = glu(z_ref)
    ubuf_ref[0, halo + tt:, :] = jnp.where(last, 0.0, glu(zn_ref))
    span = tt + 2 * halo - SUBLANES
    for s in range(1, SUBLANES):
        ubuf_ref[s, 0:span, :] = ubuf_ref[0, s:s + span, :]

    pad = (CONV_KSIZE - 1) // 2
    rows = 64
    for r in range(tt // rows):
        acc = jnp.zeros((rows, c), F32) + dwb_ref[...]
        for k in range(CONV_KSIZE):
            off = halo + r * rows - pad + k
            s = off % SUBLANES
            acc = acc + ubuf_ref[s, off - s:off - s + rows, :] * dww_ref[k:k + 1, :]
        cbuf_ref[r * rows:(r + 1) * rows, :] = acc

    g = g_ref[...]
    u = cbuf_ref[...]
    hi, lo = _split(u)
    mu = _dot(hi, g) + _dot(lo, g)
    dlt = u - mu
    var = _dot((dlt * dlt).astype(BF16), g)
    un = dlt * lax.rsqrt(var + NORM_EPS) * gnw_ref[...] + gnb_ref[...]
    act = un * _sigmoid(un)
    o_ref[...] = (_dot(act.astype(BF16), pww_ref[...]) + pwb_ref[...]).astype(o_ref.dtype)


def _conformer(zc, dw_w, dw_b, gn_w, gn_b, pw_w, pw_b, seq):
    n, c2 = zc.shape
    c = c2 // 2
    tt = CONV_TILE
    hb = tt // CONV_HALO
    n_halo_blocks = n // CONV_HALO
    dw_w = jnp.concatenate([dw_w, jnp.zeros((1, c), F32)], axis=0)
    g = _group_mean_matrix(c, CONV_GROUP_WIDTH)
    return pl.pallas_call(
        functools.partial(_conformer_kernel, tiles_per_seq=seq // tt),
        grid=(n // tt,),
        in_specs=[
            pl.BlockSpec((tt, c2), lambda i: (i, 0)),
            pl.BlockSpec((CONV_HALO, c2), lambda i: (jnp.maximum(i * hb - 1, 0), 0)),
            pl.BlockSpec((CONV_HALO, c2), lambda i: (jnp.minimum((i + 1) * hb, n_halo_blocks - 1), 0)),
            pl.BlockSpec((CONV_KSIZE + 1, c), _const),
            pl.BlockSpec((1, c), _const),
            pl.BlockSpec((1, c), _const),
            pl.BlockSpec((1, c), _const),
            pl.BlockSpec((c, c), _const),
            pl.BlockSpec((c, c), _const),
            pl.BlockSpec((1, c), _const),
        ],
        out_specs=pl.BlockSpec((tt, c), lambda i: (i, 0)),
        out_shape=jax.ShapeDtypeStruct((n, c), BF16),
        scratch_shapes=[pltpu.VMEM((SUBLANES, tt + 2 * CONV_HALO, c), F32), pltpu.VMEM((tt, c), F32)],
        compiler_params=_params("parallel"),
        name="conformer",
    )(zc, zc, zc, dw_w, dw_b.reshape(1, c), gn_w.reshape(1, c), gn_b.reshape(1, c), g,
      pw_w.astype(BF16), pw_b.reshape(1, c))


def _fpad(t):
    return -(-(t + 1) // SUBLANES) * SUBLANES


def _dft_tables(t_blk):
    fpad = _fpad(t_blk)
    t = np.arange(t_blk)
    k = np.arange(fpad)
    ang = ((k[:, None] * t[None, :]) % (2 * t_blk)) * (2.0 * np.pi / (2 * t_blk))
    live = (k <= t_blk)[:, None]
    fwd = np.concatenate([np.where(live, np.cos(ang), 0.0), np.where(live, -np.sin(ang), 0.0)], axis=0)
    ck = np.where((k == 0) | (k == t_blk), 1.0, 2.0)[:, None] / (2 * t_blk)
    inv = np.concatenate([np.where(live, ck * np.cos(ang), 0.0), np.where(live, -ck * np.sin(ang), 0.0)], axis=0).T

    return jnp.asarray(fwd, BF16), jnp.asarray(inv, BF16)


def _filter_features(length):
    lag = np.arange(2 * length) - length
    idx = np.minimum(np.abs(lag), length - 1)
    t = np.linspace(0.0, 1.0, length)[idx]
    w_ang = 2.0 * np.pi * idx / length
    bands = np.linspace(1e-4, HYENA_POS_BANDS - 1, HYENA_POS_BANDS)
    z = np.zeros((2 * length, LANES), np.float64)
    z[:, 0] = t
    z[:, 1:1 + HYENA_POS_BANDS] = np.cos(bands[None, :] * w_ang[:, None])
    z[:, 1 + HYENA_POS_BANDS:1 + 2 * HYENA_POS_BANDS] = np.sin(-bands[None, :] * w_ang[:, None])
    z[:, 1 + 2 * HYENA_POS_BANDS] = (lag > -length).astype(np.float64)
    return jnp.asarray(z, F32), jnp.asarray(z.T, F32)


def _hyena_filter_kernel(z_ref, zt_ref, w1t_ref, b1_ref, w2t_ref, b2_ref, fr_ref, w3_ref, b3_ref, dl_ref,
                         fh_ref, h_ref, prev_ref):
    s = pl.program_id(0)
    fpad = prev_ref.shape[0] // 2
    z = z_ref[...]
    fr = fr_ref[...]

    def dense_t(wt_ref, a, b_ref_):
        w_hi, w_lo = _split(wt_ref[...])
        return _dot3l(w_hi, w_lo, a) + b_ref_[...]

    hid = jnp.sin(fr * dense_t(w1t_ref, zt_ref[...], b1_ref))
    hid = jnp.sin(fr * dense_t(w2t_ref, hid, b2_ref)).T
    w_hi, w_lo = _split(w3_ref[...])
    h = _dot3(hid, w_hi, w_lo) + b3_ref[...]
    t = z[:, 0:1]
    live = z[:, 1 + 2 * HYENA_POS_BANDS:2 + 2 * HYENA_POS_BANDS]
    h = h * jnp.exp(-t * dl_ref[...]) * live
    spec = _dot(fh_ref[...], h.astype(BF16))

    @pl.when(s == 0)
    def _():
        prev_ref[...] = jnp.zeros_like(prev_ref)

    row = lax.broadcasted_iota(jnp.int32, (2 * fpad, 1), 0)
    sign = (1 - 2 * ((row % fpad) % 2)).astype(F32)
    h_ref[...] = spec + sign * prev_ref[...]
    prev_ref[...] = spec


def _hyena_filters(length, t_blk, w1, b1, w2, b2, w3, b3, freq, dft_fwd):
    nb = length // t_blk
    fpad = _fpad(t_blk)
    fh = w2.shape[0]
    oc = w3.shape[1] // 2
    c = oc // HYENA_ORDER
    z, zt = _filter_features(length)
    w1t = jnp.concatenate([w1, jnp.zeros((LANES - w1.shape[0], fh), F32)], axis=0).T
    w3d = w3.reshape(fh, HYENA_ORDER, 2, c).transpose(2, 0, 1, 3).reshape(2, fh, oc)
    b3d = b3.reshape(HYENA_ORDER, 2, c).transpose(1, 0, 2).reshape(2, 1, oc)
    deltas = np.abs(np.linspace(HYENA_MIN_DECAY, HYENA_MAX_DECAY, c))
    dl = jnp.asarray(np.tile(deltas, HYENA_ORDER)[None, :], F32)
    direction = lambda s: (jnp.where(s >= nb, 0, 1), 0, 0)
    return pl.pallas_call(
        _hyena_filter_kernel,
        grid=(2 * nb,),
        in_specs=[
            pl.BlockSpec((t_blk, LANES), lambda s: (s, 0)),
            pl.BlockSpec((LANES, t_blk), lambda s: (0, s)),
            pl.BlockSpec((fh, LANES), _const),
            pl.BlockSpec((fh, 1), _const),
            pl.BlockSpec((fh, fh), _const),
            pl.BlockSpec((fh, 1), _const),
            pl.BlockSpec((fh, 1), _const),
            pl.BlockSpec((None, fh, oc), direction),
            pl.BlockSpec((None, 1, oc), direction),
            pl.BlockSpec((1, oc), _const),
            pl.BlockSpec((2 * fpad, t_blk), _const),
        ],
        out_specs=pl.BlockSpec((None, 2 * fpad, oc), lambda s: (s, 0, 0)),
        out_shape=jax.ShapeDtypeStruct((2 * nb, 2 * fpad, oc), F32),
        scratch_shapes=[pltpu.VMEM((2 * fpad, oc), F32)],
        compiler_params=_params("arbitrary"),
        name="hyena_filter",
    )(z, zt, w1t, b1.reshape(fh, 1), w2.T, b2.reshape(fh, 1), freq.reshape(fh, 1), w3d, b3d, dl,
      dft_fwd)


def _hyena_conv_kernel(x_ref, gz_ref, sw_ref, sb_ref, gw_ref, gb_ref, bias_ref, h_ref, f_ref, g_ref,
                       o_ref, xs_ref, u_ref, y_ref, *, length, nb, short_signal, rc):
    rows, c = x_ref.shape
    t_blk = length // nb
    n_blocks = rows // t_blk
    fpad = y_ref.shape[0] // 2
    sr = min(HY_SHORT_ROWS, rows)

    def short_conv(ref, w_ref_, b_ref_, r0):
        a = ref[pl.ds(r0, sr), :].astype(F32)
        w = BF16_SUBLANES
        before = ref[pl.ds(pl.multiple_of(jnp.maximum(r0 - w, 0), w), w), :].astype(F32)[w - 1:, :]
        after = ref[pl.ds(pl.multiple_of(jnp.minimum(r0 + sr, rows - w), w), w), :].astype(F32)[:1, :]
        row = lax.broadcasted_iota(jnp.int32, (sr, 1), 0)
        pos = (r0 + row) % length
        prev = jnp.where(row == 0, before, pltpu.roll(a, 1, axis=0))
        prev = jnp.where(pos == 0, 0.0, prev)
        nxt = jnp.where(row == sr - 1, after, pltpu.roll(a, sr - 1, axis=0))
        nxt = jnp.where(pos == length - 1, 0.0, nxt)
        return prev * w_ref_[0:1, :] + a * w_ref_[1:2, :] + nxt * w_ref_[2:3, :] + b_ref_[...]

    def load_signal(r, _):
        r0 = pl.multiple_of(r * sr, sr)
        if short_signal:
            xs_ref[pl.ds(r0, sr), :] = short_conv(x_ref, sw_ref, sb_ref, r0)
        else:
            xs_ref[pl.ds(r0, sr), :] = x_ref[pl.ds(r0, sr), :].astype(F32)
        return 0

    lax.fori_loop(0, rows // sr, load_signal, 0)

    def fwd(j, _):
        xj = xs_ref[pl.ds(pl.multiple_of(j * t_blk, t_blk), t_blk), :]
        u_ref[j] = _dot(f_ref[...], xj.astype(BF16))
        return 0

    lax.fori_loop(0, n_blocks, fwd, 0)

    def out_block(sb, _):
        i = sb % nb
        first = sb - i

        def mac(r, _):
            re = pl.ds(pl.multiple_of(r * rc, SUBLANES), rc)
            im = pl.ds(pl.multiple_of(fpad + r * rc, SUBLANES), rc)
            are = jnp.zeros((rc, c), F32)
            aim = jnp.zeros((rc, c), F32)
            for j in range(nb):
                d = i - j + nb
                hre, him = h_ref[d, re, :], h_ref[d, im, :]
                ure, uim = u_ref[first + j, re, :], u_ref[first + j, im, :]
                are = are + hre * ure - him * uim
                aim = aim + hre * uim + him * ure
            y_ref[re, :] = are
            y_ref[im, :] = aim
            return 0

        lax.fori_loop(0, fpad // rc, mac, 0)
        blk = pl.ds(pl.multiple_of(sb * t_blk, t_blk), t_blk)
        conv = _dot(g_ref[...], y_ref[...].astype(BF16))
        xs_ref[blk, :] = conv + xs_ref[blk, :] * bias_ref[...]
        return 0

    lax.fori_loop(0, n_blocks, out_block, 0)

    def gate(r, _):
        r0 = pl.multiple_of(r * sr, sr)
        o_ref[pl.ds(r0, sr), :] = (xs_ref[pl.ds(r0, sr), :] * short_conv(gz_ref, gw_ref, gb_ref, r0)).astype(o_ref.dtype)
        return 0

    lax.fori_loop(0, rows // sr, gate, 0)


def _hyena_conv(x, x_col, zh, gate_col, short_w, short_b, bias, spectra, order, length, t_blk,
                dft_fwd, dft_inv, short_signal):
    c = bias.shape[0]
    n = zh.shape[0]
    nb = length // t_blk
    fpad = _fpad(t_blk)
    rc = max(r for r in (8, 24, 40) if fpad % r == 0)
    rows = min(max(length, HY_STEP_ROWS), n)
    big = _SINGLE if 2 * nb * 2 * fpad * c * 4 > HY_SINGLE_BUFFER_BYTES else None
    return pl.pallas_call(
        functools.partial(_hyena_conv_kernel, length=length, nb=nb, short_signal=short_signal, rc=rc),
        grid=(n // rows,),
        in_specs=[
            pl.BlockSpec((rows, c), lambda b: (b, x_col)),
            pl.BlockSpec((rows, c), lambda b: (b, gate_col)),
            pl.BlockSpec((3, c), lambda b: (0, x_col)),
            pl.BlockSpec((1, c), lambda b: (0, x_col)),
            pl.BlockSpec((3, c), lambda b: (0, gate_col)),
            pl.BlockSpec((1, c), lambda b: (0, gate_col)),
            pl.BlockSpec((1, c), lambda b: (0, 0)),
            pl.BlockSpec((2 * nb, 2 * fpad, c), lambda b: (0, 0, order), pipeline_mode=big),
            pl.BlockSpec((2 * fpad, t_blk), _const),
            pl.BlockSpec((t_blk, 2 * fpad), _const),
        ],
        out_specs=pl.BlockSpec((rows, c), lambda b: (b, 0)),
        out_shape=jax.ShapeDtypeStruct((n, c), BF16),
        scratch_shapes=[
            pltpu.VMEM((rows, c), F32),
            pltpu.VMEM((rows // t_blk, 2 * fpad, c), F32),
            pltpu.VMEM((2 * fpad, c), F32),
        ],
        compiler_params=_params("arbitrary"),
        name="hyena_conv",
    )(x, zh, short_w, short_b.reshape(1, -1), short_w, short_b.reshape(1, -1), bias.reshape(1, c), spectra,
      dft_fwd, dft_inv)


def _hyena(zh, short_w, short_b, hy_bias, filt, length):
    t_blk = min(HY_MAX_BLOCK, length)
    dft_fwd, dft_inv = _dft_tables(t_blk)
    spectra = _hyena_filters(length, t_blk, *filt, dft_fwd)
    u1 = _hyena_conv(zh, 0, zh, 1, short_w, short_b, hy_bias[0], spectra, 0, length, t_blk,
                     dft_fwd, dft_inv, True)
    return _hyena_conv(u1, 0, zh, 2, short_w, short_b, hy_bias[1], spectra, 1, length, t_blk,
                       dft_fwd, dft_inv, False)


def _mix_ffn_kernel(x_ref, xp_ref, xn_ref, a_ref, ap_ref, an_ref, c_ref, cp_ref, cn_ref, h_ref, hp_ref, hn_ref,
                    mod_ref, wo_ref, n2_ref, wup_ref, dww_ref, dwb_ref, wdn_ref, fn_ref,
                    o_ref, p_ref, acc_ref, *, seq, final):
    i = pl.program_id(0)
    tm, d = x_ref.shape
    ffn = wdn_ref.shape[0]
    nc = ffn // FFN_CHUNKS
    halo = FFN_HALO
    seg = min(seq, tm)
    nseg = tm // seg
    inside = seq > tm

    def mixed(refs, rows):
        return jnp.concatenate([r[rows, :] for r in refs], axis=1)

    mix_in = mixed((a_ref, c_ref, h_ref), slice(None))
    x_in = x_ref[...]
    if inside:
        lo, hi = slice(BF16_SUBLANES - halo, BF16_SUBLANES), slice(0, halo)
        mix_in = jnp.concatenate([mixed((ap_ref, cp_ref, hp_ref), lo), mix_in,
                                  mixed((an_ref, cn_ref, hn_ref), hi)], axis=0)
        x_in = jnp.concatenate([xp_ref[...], x_in, xn_ref[...]], axis=0)
    x1 = x_in + mod_ref[2:3, :] * _dot(mix_in, wo_ref[...])

    def norm_mod(x):
        ms = jnp.mean(x * x, axis=-1, keepdims=True)
        h = x * lax.rsqrt(ms + NORM_EPS) * n2_ref[...]
        return h * (1.0 + mod_ref[4:5, :]) + mod_ref[3:4, :]

    zeros = jnp.zeros((halo, d), F32)
    if inside:
        tiles = seq // tm
        hx = norm_mod(x1)
        before = jnp.where(i % tiles == 0, 0.0, hx[:halo])
        after = jnp.where(i % tiles == tiles - 1, 0.0, hx[halo + tm:])
        hm = hx[halo:halo + tm]
        x1 = x1[halo:halo + tm]
    else:
        before, after = zeros, zeros
        hm = norm_mod(x1)
    parts = [before]
    for s in range(nseg):
        parts += [hm[s * seg:(s + 1) * seg], after if s == nseg - 1 else zeros]
    hb = jnp.concatenate(parts, axis=0).astype(BF16)

    def dwconv(pr, col):
        w = dww_ref[:, col:col + nc]
        out = []
        for s in range(nseg):
            o = halo + s * (seg + halo)
            out.append(pr[o - 1:o - 1 + seg, :] * w[0:1] + pr[o:o + seg, :] * w[1:2]
                       + pr[o + 1:o + 1 + seg, :] * w[2:3] + dwb_ref[:, col:col + nc])
        return out[0] if nseg == 1 else jnp.concatenate(out, axis=0)

    for c in range(FFN_CHUNKS):
        c0 = c * nc
        p_ref[2 * c] = _dot(hb, wup_ref[:, c0:c0 + nc])
        p_ref[2 * c + 1] = _dot(hb, wup_ref[:, ffn + c0:ffn + c0 + nc])
        val = dwconv(p_ref.at[2 * c], c0)
        gate = dwconv(p_ref.at[2 * c + 1], ffn + c0)
        act = (gate * _sigmoid(gate) * val).astype(BF16)
        part = _dot(act, wdn_ref[c0:c0 + nc, :])
        if c == 0:
            acc_ref[...] = part
        else:
            acc_ref[...] += part

    out = x1 + mod_ref[5:6, :] * acc_ref[...]
    if final:
        ms = jnp.mean(out * out, axis=-1, keepdims=True)
        out = out * lax.rsqrt(ms + NORM_EPS) * fn_ref[...]
    o_ref[...] = out


def _mix_ffn(x, attn, conv, hy, mod_l, cond_base, w_o, norm2, w_up, dw_w, dw_b, w_down, final_norm, seq, final):
    n, d = x.shape
    tm = TOKEN_TILE
    ffn = w_down.shape[0]
    nc = ffn // FFN_CHUNKS

    def tile_and_halos(arr, halo):
        w = arr.shape[1]
        per_tile = tm // halo
        last = n // halo - 1
        return [pl.BlockSpec((tm, w), lambda i: (i, 0)),
                pl.BlockSpec((halo, w), lambda i: (jnp.maximum(i * per_tile - 1, 0), 0)),
                pl.BlockSpec((halo, w), lambda i: (jnp.minimum((i + 1) * per_tile, last), 0))]

    return pl.pallas_call(
        functools.partial(_mix_ffn_kernel, seq=seq, final=final),
        grid=(n // tm,),
        in_specs=[
            *tile_and_halos(x, FFN_HALO),
            *tile_and_halos(attn, BF16_SUBLANES),
            *tile_and_halos(conv, BF16_SUBLANES),
            *tile_and_halos(hy, BF16_SUBLANES),
            _mod_spec(d, tm, seq, cond_base),
            pl.BlockSpec(w_o.shape, _const, pipeline_mode=_SINGLE),
            pl.BlockSpec((1, d), _const),
            pl.BlockSpec(w_up.shape, _const, pipeline_mode=_SINGLE),
            pl.BlockSpec((3, 2 * ffn), _const),
            pl.BlockSpec((1, 2 * ffn), _const),
            pl.BlockSpec(w_down.shape, _const, pipeline_mode=_SINGLE),
            pl.BlockSpec((1, d), _const),
        ],
        out_specs=pl.BlockSpec((tm, d), lambda i: (i, 0)),
        out_shape=jax.ShapeDtypeStruct((n, d), F32),
        scratch_shapes=[
            pltpu.VMEM((2 * FFN_CHUNKS, tm + (tm // min(seq, tm) + 1) * FFN_HALO, nc), F32),
            pltpu.VMEM((tm, d), F32),
        ],
        compiler_params=_params("parallel"),
        name="mix_ffn",
    )(x, x, x, attn, attn, attn, conv, conv, conv, hy, hy, hy, mod_l, w_o, norm2.reshape(1, d), w_up, dw_w,
      dw_b.reshape(1, -1), w_down, final_norm.reshape(1, d))


def kernel(x_prompt, x_sample, cache_k, cache_v, c, c_ctx, norm1, norm2, w_mod, b_mod, w_in, q_norm, k_norm,
           conv_dw_w, conv_dw_b, conv_gn_w, conv_gn_b, conv_pw_w, conv_pw_b, hy_short_w, hy_short_b,
           hy_f_w1, hy_f_b1, hy_f_w2, hy_f_b2, hy_f_w3, hy_f_b3, hy_freq, hy_bias, w_o, ffn_w_up,
           ffn_dw_w, ffn_dw_b, ffn_w_down, final_norm):
    batch, ctx_seq, d = x_prompt.shape
    dec_batch, lat_seq, _ = x_sample.shape
    past = cache_k.shape[2]
    depth = norm1.shape[0]
    conv_w = conv_pw_w.shape[1]
    hy_w = hy_bias.shape[2]
    assert 1 + dec_batch <= SUBLANES

    cond = jnp.concatenate([c_ctx[None, :], c, jnp.zeros((SUBLANES - 1 - dec_batch, d), F32)], axis=0)
    mod = _modulation(cond, w_mod, b_mod).transpose(0, 2, 1, 3)

    xs = [x_prompt.reshape(batch * ctx_seq, d), x_sample.reshape(dec_batch * lat_seq, d)]
    seqs = [ctx_seq, lat_seq]
    cond_bases = [0, 1]
    new_k, new_v = [], []
    for l in range(depth):
        w_in_l = w_in[l].astype(BF16)
        w_o_l = w_o[l].astype(BF16)
        w_up_l = ffn_w_up[l].astype(BF16)
        w_down_l = ffn_w_down[l].astype(BF16)
        filt = (hy_f_w1[l], hy_f_b1[l], hy_f_w2[l], hy_f_b2[l], hy_f_w3[l], hy_f_b3[l], hy_freq[l])
        for path in range(2):
            x, seq, base = xs[path], seqs[path], cond_bases[path]
            qt, k, kb, v, vt, zc, zh = _inproj(x, mod[l], base, norm1[l], w_in_l, q_norm[l], k_norm[l],
                                               seq, path == 1, 2 * conv_w, (HYENA_ORDER + 1) * hy_w)
            if path == 0:
                new_k.append(k.reshape(batch, ctx_seq, N_KV_HEADS, HEAD_DIM))
                new_v.append(v.reshape(batch, ctx_seq, N_KV_HEADS, HEAD_DIM))
                cache = None
            else:
                cache = (cache_k[:, l].reshape(dec_batch, past, KV_WIDTH).astype(BF16),
                         cache_v[:, l].reshape(dec_batch, past, KV_WIDTH).transpose(0, 2, 1).astype(BF16))
            attn = _attention(qt, kb, vt, seq, cache)
            conv = _conformer(zc, conv_dw_w[l], conv_dw_b[l], conv_gn_w[l], conv_gn_b[l], conv_pw_w[l],
                              conv_pw_b[l], seq)
            hy = _hyena(zh, hy_short_w[l], hy_short_b[l], hy_bias[l], filt, seq)
            xs[path] = _mix_ffn(x, attn, conv, hy, mod[l], base, w_o_l, norm2[l], w_up_l, ffn_dw_w[l],
                                ffn_dw_b[l], w_down_l, final_norm, seq, l == depth - 1)

    y_prompt = xs[0].reshape(batch, ctx_seq, d)
    y_sample = xs[1].reshape(dec_batch, lat_seq, d)
    return y_prompt, y_sample, jnp.stack(new_k, axis=1), jnp.stack(new_v, axis=1)
```

```python
import functools
import math

import numpy as np
import jax
import jax.numpy as jnp
from jax import lax
from jax.experimental import pallas as pl
from jax.experimental.pallas import tpu as pltpu

F32 = jnp.float32
BF16 = jnp.bfloat16

GRID_W = 64
N_HEADS = 8
N_KV_HEADS = 2
HEAD_DIM = 64
GROUP = N_HEADS // N_KV_HEADS
ATTN_WIDTH = N_HEADS * HEAD_DIM
KV_WIDTH = N_KV_HEADS * HEAD_DIM
CONV_GROUP_WIDTH = 64
CONV_KSIZE = 31
HYENA_ORDER = 2
HYENA_POS_BANDS = 16
HYENA_MIN_DECAY = math.log(1e-2) / 1.5
HYENA_MAX_DECAY = math.log(1e-2) / 0.3
ROPE_BASE = 10000.0
NORM_EPS = 1e-6

LANES = 128
SUBLANES = 8
BF16_SUBLANES = 16
VMEM_LIMIT = 56 * 1024 * 1024

TOKEN_TILE = 512
INPROJ_SLAB = 256
CONV_TILE = 256
CONV_HALO = 16
ATTN_Q_TILE = 1024
ATTN_K_CHUNK = 512
HY_MAX_BLOCK = 512
HY_SHORT_ROWS = 512
HY_STEP_ROWS = 2048
HY_SINGLE_BUFFER_BYTES = 8 * 1024 * 1024
FFN_HALO = 8
FFN_STRIP = 256

_SINGLE = pl.Buffered(1)


def _params(*sem):
    return pltpu.CompilerParams(dimension_semantics=sem, vmem_limit_bytes=VMEM_LIMIT)


def _split(x):
    hi = x.astype(BF16)
    lo = (x - hi.astype(F32)).astype(BF16)
    return hi, lo


def _dot(a, b):
    return jnp.dot(a, b, preferred_element_type=F32)


def _dot3(a, b_hi, b_lo):
    a_hi, a_lo = _split(a)
    return _dot(a_hi, b_hi) + _dot(a_hi, b_lo) + _dot(a_lo, b_hi)


def _dot3l(a_hi, a_lo, b):
    b_hi, b_lo = _split(b)
    return _dot(a_hi, b_hi) + _dot(a_lo, b_hi) + _dot(a_hi, b_lo)


def _sigmoid(x):
    return 1.0 / (1.0 + jnp.exp(-x))


def _const(*_):
    return (0, 0)


def _mod_kernel(cond_ref, w_ref, b_ref, o_ref):
    c = cond_ref[...]
    s = c * _sigmoid(c)
    o_ref[...] = _dot(s.astype(BF16), w_ref[...].astype(BF16)) + b_ref[...]


def _modulation(cond, w_mod, b_mod):
    depth, d, _ = w_mod.shape
    return pl.pallas_call(
        _mod_kernel,
        grid=(depth, 6),
        in_specs=[
            pl.BlockSpec((SUBLANES, d), lambda l, j: (0, 0)),
            pl.BlockSpec((None, d, d), lambda l, j: (l, 0, j)),
            pl.BlockSpec((None, 1, d), lambda l, j: (l, 0, j)),
        ],
        out_specs=pl.BlockSpec((None, None, SUBLANES, d), lambda l, j: (l, j, 0, 0)),
        out_shape=jax.ShapeDtypeStruct((depth, 6, SUBLANES, d), F32),
        compiler_params=_params("arbitrary", "arbitrary"),
        name="modulation",
    )(cond, w_mod, b_mod.reshape(depth, 1, 6 * d))


def _mod_spec(d, tile, seq, cond_base):
    per_seq = seq // tile if seq >= tile else 0
    if cond_base == 0:
        return pl.BlockSpec((None, 6, d), lambda i: (0, 0, 0))
    return pl.BlockSpec((None, 6, d), lambda i: (cond_base + i // per_seq, 0, 0))


def _inproj_kernel(x_ref, mod_ref, n1_ref, w_ref, qn_ref, kn_ref, g_ref, cos_ref, sin_ref,
                   qt_ref, k_ref, kb_ref, v_ref, vt_ref, zc_ref, zh_ref, *, rotary):
    x = x_ref[...]
    ms = jnp.mean(x * x, axis=-1, keepdims=True)
    h = x * lax.rsqrt(ms + NORM_EPS) * n1_ref[...]
    h = h * (1.0 + mod_ref[1:2, :]) + mod_ref[0:1, :]
    proj = _dot(h.astype(BF16), w_ref[...])

    def head_norm(t, w_ref_):
        width = t.shape[1]
        msq = _dot((t * t).astype(BF16), g_ref[:width, :width])
        t = t * lax.rsqrt(msq + NORM_EPS) * w_ref_[:, :width]
        if rotary:
            lane = lax.broadcasted_iota(jnp.int32, (1, width), 1)
            partner = jnp.where((lane % 32) < 16, pltpu.roll(t, width - 16, axis=1), pltpu.roll(t, 16, axis=1))
            t = t * cos_ref[:, :width] + partner * sin_ref[:, :width]
        return t

    scale = HEAD_DIM ** -0.5 * math.log2(math.e)
    slab = g_ref.shape[0]
    for c in range(ATTN_WIDTH // slab):
        sl = slice(c * slab, (c + 1) * slab)
        qn = head_norm(proj[:, sl], qn_ref) * scale
        qt_ref[sl, :] = qn.T.astype(qt_ref.dtype)

    kn = head_norm(proj[:, ATTN_WIDTH:ATTN_WIDTH + KV_WIDTH], kn_ref)
    k_ref[...] = kn
    kb_ref[...] = kn.astype(kb_ref.dtype)
    o = ATTN_WIDTH + KV_WIDTH
    v = proj[:, o:o + KV_WIDTH]
    v_ref[...] = v
    vt_ref[...] = v.T.astype(vt_ref.dtype)
    o += KV_WIDTH
    zc_ref[...] = proj[:, o:o + zc_ref.shape[1]].astype(zc_ref.dtype)
    o += zc_ref.shape[1]
    zh_ref[...] = proj[:, o:o + zh_ref.shape[1]].astype(zh_ref.dtype)


def _rope_tables(length, width):
    pos = np.arange(length)
    rows, cols = pos // GRID_W, pos % GRID_W
    half = HEAD_DIM // 2
    inv = ROPE_BASE ** (-np.arange(0, half, 2, dtype=np.float64) / half)
    d = np.arange(HEAD_DIM)
    p = np.where(d[None, :] < half, rows[:, None], cols[:, None]).astype(np.float64)
    ang = p * inv[d % (half // 2)][None, :]
    sign = np.where((d % half) < half // 2, -1.0, 1.0)[None, :]
    cos = np.tile(np.cos(ang), (1, width // HEAD_DIM))
    sin = np.tile(np.sin(ang) * sign, (1, width // HEAD_DIM))
    return jnp.asarray(cos, F32), jnp.asarray(sin, F32)


def _group_mean_matrix(width, group):
    idx = np.arange(width) // group
    return jnp.asarray((idx[:, None] == idx[None, :]).astype(np.float32) / group, BF16)


def _inproj(x, mod_l, cond_base, norm1, w_in, q_norm, k_norm, seq, rotary, zc_w, zh_w):
    n, d = x.shape
    tm = TOKEN_TILE
    per_seq = max(seq // tm, 1)
    slab = INPROJ_SLAB
    cos, sin = _rope_tables(seq if rotary else tm, slab)
    g = _group_mean_matrix(slab, HEAD_DIM)
    tab = (lambda i: (i % per_seq, 0)) if rotary else _const
    row = lambda i: (i, 0)
    col = lambda i: (0, i)
    return pl.pallas_call(
        functools.partial(_inproj_kernel, rotary=rotary),
        grid=(n // tm,),
        in_specs=[
            pl.BlockSpec((tm, d), row),
            _mod_spec(d, tm, seq, cond_base),
            pl.BlockSpec((1, d), _const),
            pl.BlockSpec(w_in.shape, _const),
            pl.BlockSpec((1, slab), _const),
            pl.BlockSpec((1, slab), _const),
            pl.BlockSpec((slab, slab), _const),
            pl.BlockSpec((tm, slab), tab),
            pl.BlockSpec((tm, slab), tab),
        ],
        out_specs=[
            pl.BlockSpec((ATTN_WIDTH, tm), col),
            pl.BlockSpec((tm, KV_WIDTH), row),
            pl.BlockSpec((tm, KV_WIDTH), row),
            pl.BlockSpec((tm, KV_WIDTH), row),
            pl.BlockSpec((KV_WIDTH, tm), col),
            pl.BlockSpec((tm, zc_w), row),
            pl.BlockSpec((tm, zh_w), row),
        ],
        out_shape=[
            jax.ShapeDtypeStruct((ATTN_WIDTH, n), BF16),
            jax.ShapeDtypeStruct((n, KV_WIDTH), F32),
            jax.ShapeDtypeStruct((n, KV_WIDTH), BF16),
            jax.ShapeDtypeStruct((n, KV_WIDTH), F32),
            jax.ShapeDtypeStruct((KV_WIDTH, n), BF16),
            jax.ShapeDtypeStruct((n, zc_w), BF16),
            jax.ShapeDtypeStruct((n, zh_w), BF16),
        ],
        compiler_params=_params("parallel"),
        name="inproj",
    )(x, mod_l, norm1.reshape(1, d), w_in,
      jnp.tile(q_norm, slab // HEAD_DIM).reshape(1, slab),
      jnp.tile(k_norm, slab // HEAD_DIM).reshape(1, slab), g, cos, sin)


def _attn_kernel(*refs, chunks, cache_chunks):
    if cache_chunks:
        qt_ref, k_ref, vt_ref, ck_ref, cvt_ref, o_ref = refs
    else:
        qt_ref, k_ref, vt_ref, o_ref = refs
    tq = qt_ref.shape[1]
    rows = GROUP * HEAD_DIM
    heads = qt_ref.shape[0] // rows

    def ones_rows(size):
        r = lax.broadcasted_iota(jnp.int32, (BF16_SUBLANES, size), 0)
        return jnp.where(r == 0, 1.0, 0.0).astype(BF16)

    def update(s, v, m, acc):
        m_new = jnp.maximum(m, jnp.max(s, axis=0, keepdims=True))
        alpha = jnp.exp2(m - m_new)
        p = jnp.exp2(s - m_new).astype(BF16)
        v_aug = jnp.concatenate([v, ones_rows(v.shape[1])], axis=0)
        return m_new, acc * alpha + _dot(v_aug, p)

    for hh in range(heads):
        h = hh if heads == N_KV_HEADS else pl.program_id(1)
        q4 = jnp.concatenate([qt_ref[hh * rows + g * HEAD_DIM:hh * rows + (g + 1) * HEAD_DIM, :]
                              for g in range(GROUP)], axis=1)
        zero = jnp.zeros_like(q4)
        q_pad = jnp.where(h == 0, jnp.concatenate([q4, zero], axis=0), jnp.concatenate([zero, q4], axis=0))
        vrows = slice(hh * HEAD_DIM, (hh + 1) * HEAD_DIM)

        m = jnp.full((1, GROUP * tq), -1e30, F32)
        acc = jnp.zeros((HEAD_DIM + BF16_SUBLANES, GROUP * tq), F32)
        for start, size in chunks:
            m, acc = update(_dot(k_ref[start:start + size, :], q_pad), vt_ref[vrows, start:start + size], m, acc)
        for start, size in cache_chunks:
            m, acc = update(_dot(ck_ref[start:start + size, :], q_pad), cvt_ref[vrows, start:start + size], m, acc)
        out = acc[:HEAD_DIM] / acc[HEAD_DIM:HEAD_DIM + 1]
        out = jnp.concatenate([out[:, g * tq:(g + 1) * tq] for g in range(GROUP)], axis=0)
        o_ref[:, hh * rows:(hh + 1) * rows] = out.T.astype(o_ref.dtype)


def _chunks(n):
    return [(s, min(ATTN_K_CHUNK, n - s)) for s in range(0, n, ATTN_K_CHUNK)]


def _attention(qt, kb, vt, seq, cache=None):
    n = kb.shape[0]
    nb = n // seq
    tq = min(ATTN_Q_TILE, seq)
    nq = seq // tq
    hps = N_KV_HEADS if seq <= ATTN_K_CHUNK else 1
    rows = hps * GROUP * HEAD_DIM
    in_specs = [
        pl.BlockSpec((rows, tq), lambda b, h, i: (h, b * nq + i)),
        pl.BlockSpec((seq, KV_WIDTH), lambda b, h, i: (b, 0)),
        pl.BlockSpec((hps * HEAD_DIM, seq), lambda b, h, i: (h, b)),
    ]
    args = [qt, kb, vt]
    cache_chunks = []
    if cache is not None:
        past = cache[0].shape[1]
        in_specs += [
            pl.BlockSpec((None, past, KV_WIDTH), lambda b, h, i: (b, 0, 0)),
            pl.BlockSpec((None, hps * HEAD_DIM, past), lambda b, h, i: (b, h, 0)),
        ]
        args += list(cache)
        cache_chunks = _chunks(past)
    return pl.pallas_call(
        functools.partial(_attn_kernel, chunks=_chunks(seq), cache_chunks=cache_chunks),
        grid=(nb, N_KV_HEADS // hps, nq),
        in_specs=in_specs,
        out_specs=pl.BlockSpec((tq, rows), lambda b, h, i: (b * nq + i, h)),
        out_shape=jax.ShapeDtypeStruct((n, ATTN_WIDTH), BF16),
        compiler_params=_params("parallel", "parallel", "parallel"),
        name="attention",
    )(*args)


def _conformer_kernel(z_ref, zp_ref, zn_ref, dww_ref, dwb_ref, gnw_ref, gnb_ref, g_ref, pww_ref, pwb_ref,
                      o_ref, ubuf_ref, cbuf_ref, *, tiles_per_seq):
    i = pl.program_id(0)
    tt, c = o_ref.shape
    halo = CONV_HALO
    first = (i % tiles_per_seq) == 0
    last = (i % tiles_per_seq) == tiles_per_seq - 1

    def glu(ref):
        z = ref[...].astype(F32)
        return z[:, :c] * _sigmoid(z[:, c:])

    ubuf_ref[0, 0:halo, :] = jnp.where(first, 0.0, glu(zp_ref))
    ubuf_ref[0, halo:halo + tt, :] = glu(z_ref)
    ubuf_ref[0, halo + tt:, :] = jnp.where(last, 0.0, glu(zn_ref))
    span = tt + 2 * halo - SUBLANES
    for s in range(1, SUBLANES):
        ubuf_ref[s, 0:span, :] = ubuf_ref[0, s:s + span, :]

    pad = (CONV_KSIZE - 1) // 2
    rows = 64
    for r in range(tt // rows):
        acc = jnp.zeros((rows, c), F32) + dwb_ref[...]
        for k in range(CONV_KSIZE):
            off = halo + r * rows - pad + k
            s = off % SUBLANES
            acc = acc + ubuf_ref[s, off - s:off - s + rows, :] * dww_ref[k:k + 1, :]
        cbuf_ref[r * rows:(r + 1) * rows, :] = acc

    g = g_ref[...]
    u = cbuf_ref[...]
    hi, lo = _split(u)
    mu = _dot(hi, g) + _dot(lo, g)
    dlt = u - mu
    var = _dot((dlt * dlt).astype(BF16), g)
    un = dlt * lax.rsqrt(var + NORM_EPS) * gnw_ref[...] + gnb_ref[...]
    act = un * _sigmoid(un)
    o_ref[...] = (_dot(act.astype(BF16), pww_ref[...]) + pwb_ref[...]).astype(o_ref.dtype)


def _conformer(zc, dw_w, dw_b, gn_w, gn_b, pw_w, pw_b, seq):
    n, c2 = zc.shape
    c = c2 // 2
    tt = CONV_TILE
    hb = tt // CONV_HALO
    n_halo_blocks = n // CONV_HALO
    dw_w = jnp.concatenate([dw_w, jnp.zeros((1, c), F32)], axis=0)
    g = _group_mean_matrix(c, CONV_GROUP_WIDTH)
    return pl.pallas_call(
        functools.partial(_conformer_kernel, tiles_per_seq=seq // tt),
        grid=(n // tt,),
        in_specs=[
            pl.BlockSpec((tt, c2), lambda i: (i, 0)),
            pl.BlockSpec((CONV_HALO, c2), lambda i: (jnp.maximum(i * hb - 1, 0), 0)),
            pl.BlockSpec((CONV_HALO, c2), lambda i: (jnp.minimum((i + 1) * hb, n_halo_blocks - 1), 0)),
            pl.BlockSpec((CONV_KSIZE + 1, c), _const),
            pl.BlockSpec((1, c), _const),
            pl.BlockSpec((1, c), _const),
            pl.BlockSpec((1, c), _const),
            pl.BlockSpec((c, c), _const),
            pl.BlockSpec((c, c), _const),
            pl.BlockSpec((1, c), _const),
        ],
        out_specs=pl.BlockSpec((tt, c), lambda i: (i, 0)),
        out_shape=jax.ShapeDtypeStruct((n, c), BF16),
        scratch_shapes=[pltpu.VMEM((SUBLANES, tt + 2 * CONV_HALO, c), F32), pltpu.VMEM((tt, c), F32)],
        compiler_params=_params("parallel"),
        name="conformer",
    )(zc, zc, zc, dw_w, dw_b.reshape(1, c), gn_w.reshape(1, c), gn_b.reshape(1, c), g,
      pw_w.astype(BF16), pw_b.reshape(1, c))


def _fpad(t):
    return -(-(t + 1) // SUBLANES) * SUBLANES


def _dft_tables(t_blk):
    fpad = _fpad(t_blk)
    t = np.arange(t_blk)
    k = np.arange(fpad)
    ang = ((k[:, None] * t[None, :]) % (2 * t_blk)) * (2.0 * np.pi / (2 * t_blk))
    live = (k <= t_blk)[:, None]
    fwd = np.concatenate([np.where(live, np.cos(ang), 0.0), np.where(live, -np.sin(ang), 0.0)], axis=0)
    ck = np.where((k == 0) | (k == t_blk), 1.0, 2.0)[:, None] / (2 * t_blk)
    inv = np.concatenate([np.where(live, ck * np.cos(ang), 0.0), np.where(live, -ck * np.sin(ang), 0.0)], axis=0).T

    return jnp.asarray(fwd, BF16), jnp.asarray(inv, BF16)


def _filter_features(length):
    lag = np.arange(2 * length) - length
    idx = np.minimum(np.abs(lag), length - 1)
    t = np.linspace(0.0, 1.0, length)[idx]
    w_ang = 2.0 * np.pi * idx / length
    bands = np.linspace(1e-4, HYENA_POS_BANDS - 1, HYENA_POS_BANDS)
    z = np.zeros((2 * length, LANES), np.float64)
    z[:, 0] = t
    z[:, 1:1 + HYENA_POS_BANDS] = np.cos(bands[None, :] * w_ang[:, None])
    z[:, 1 + HYENA_POS_BANDS:1 + 2 * HYENA_POS_BANDS] = np.sin(-bands[None, :] * w_ang[:, None])
    z[:, 1 + 2 * HYENA_POS_BANDS] = (lag > -length).astype(np.float64)
    return jnp.asarray(z, F32), jnp.asarray(z.T, F32)


def _hyena_filter_kernel(z_ref, zt_ref, w1t_ref, b1_ref, w2t_ref, b2_ref, fr_ref, w3_ref, b3_ref, dl_ref,
                         fh_ref, h_ref, prev_ref):
    s = pl.program_id(0)
    fpad = prev_ref.shape[0] // 2
    z = z_ref[...]
    fr = fr_ref[...]

    def dense_t(wt_ref, a, b_ref_):
        w_hi, w_lo = _split(wt_ref[...])
        return _dot3l(w_hi, w_lo, a) + b_ref_[...]

    hid = jnp.sin(fr * dense_t(w1t_ref, zt_ref[...], b1_ref))
    hid = jnp.sin(fr * dense_t(w2t_ref, hid, b2_ref)).T
    w_hi, w_lo = _split(w3_ref[...])
    h = _dot3(hid, w_hi, w_lo) + b3_ref[...]
    t = z[:, 0:1]
    live = z[:, 1 + 2 * HYENA_POS_BANDS:2 + 2 * HYENA_POS_BANDS]
    h = h * jnp.exp(-t * dl_ref[...]) * live
    spec = _dot(fh_ref[...], h.astype(BF16))

    @pl.when(s == 0)
    def _():
        prev_ref[...] = jnp.zeros_like(prev_ref)

    row = lax.broadcasted_iota(jnp.int32, (2 * fpad, 1), 0)
    sign = (1 - 2 * ((row % fpad) % 2)).astype(F32)
    h_ref[...] = spec + sign * prev_ref[...]
    prev_ref[...] = spec


def _hyena_filters(length, t_blk, w1, b1, w2, b2, w3, b3, freq, dft_fwd):
    nb = length // t_blk
    fpad = _fpad(t_blk)
    fh = w2.shape[0]
    oc = w3.shape[1] // 2
    c = oc // HYENA_ORDER
    z, zt = _filter_features(length)
    w1t = jnp.concatenate([w1, jnp.zeros((LANES - w1.shape[0], fh), F32)], axis=0).T
    w3d = w3.reshape(fh, HYENA_ORDER, 2, c).transpose(2, 0, 1, 3).reshape(2, fh, oc)
    b3d = b3.reshape(HYENA_ORDER, 2, c).transpose(1, 0, 2).reshape(2, 1, oc)
    deltas = np.abs(np.linspace(HYENA_MIN_DECAY, HYENA_MAX_DECAY, c))
    dl = jnp.asarray(np.tile(deltas, HYENA_ORDER)[None, :], F32)
    direction = lambda s: (jnp.where(s >= nb, 0, 1), 0, 0)
    return pl.pallas_call(
        _hyena_filter_kernel,
        grid=(2 * nb,),
        in_specs=[
            pl.BlockSpec((t_blk, LANES), lambda s: (s, 0)),
            pl.BlockSpec((LANES, t_blk), lambda s: (0, s)),
            pl.BlockSpec((fh, LANES), _const),
            pl.BlockSpec((fh, 1), _const),
            pl.BlockSpec((fh, fh), _const),
            pl.BlockSpec((fh, 1), _const),
            pl.BlockSpec((fh, 1), _const),
            pl.BlockSpec((None, fh, oc), direction),
            pl.BlockSpec((None, 1, oc), direction),
            pl.BlockSpec((1, oc), _const),
            pl.BlockSpec((2 * fpad, t_blk), _const),
        ],
        out_specs=pl.BlockSpec((None, 2 * fpad, oc), lambda s: (s, 0, 0)),
        out_shape=jax.ShapeDtypeStruct((2 * nb, 2 * fpad, oc), F32),
        scratch_shapes=[pltpu.VMEM((2 * fpad, oc), F32)],
        compiler_params=_params("arbitrary"),
        name="hyena_filter",
    )(z, zt, w1t, b1.reshape(fh, 1), w2.T, b2.reshape(fh, 1), freq.reshape(fh, 1), w3d, b3d, dl,
      dft_fwd)


def _hyena_conv_kernel(x_ref, gz_ref, sw_ref, sb_ref, gw_ref, gb_ref, bias_ref, h_ref, f_ref, g_ref,
                       o_ref, xs_ref, u_ref, y_ref, *, length, nb, short_signal, rc):
    rows, c = x_ref.shape
    t_blk = length // nb
    n_blocks = rows // t_blk
    fpad = y_ref.shape[0] // 2
    sr = min(HY_SHORT_ROWS, rows)

    def short_conv(ref, w_ref_, b_ref_, r0):
        a = ref[pl.ds(r0, sr), :].astype(F32)
        w = BF16_SUBLANES
        before = ref[pl.ds(pl.multiple_of(jnp.maximum(r0 - w, 0), w), w), :].astype(F32)[w - 1:, :]
        after = ref[pl.ds(pl.multiple_of(jnp.minimum(r0 + sr, rows - w), w), w), :].astype(F32)[:1, :]
        prev = pltpu.roll(a, 1, axis=0)
        nxt = pltpu.roll(a, sr - 1, axis=0)
        if length >= sr:
            before = jnp.where(r0 % length == 0, 0.0, before)
            after = jnp.where((r0 + sr) % length == 0, 0.0, after)
            starts = {0: before}
            ends = {sr: after}
        else:
            starts = {s: 0.0 for s in range(0, sr, length)}
            ends = {s + length: 0.0 for s in range(0, sr, length)}
        sub = lax.broadcasted_iota(jnp.int32, (SUBLANES, 1), 0)

        def patched(v, at, pick):
            pieces, done = [], 0
            for r, val in sorted(at.items()):
                g0 = (r if pick == 0 else r - SUBLANES)
                if g0 > done:
                    pieces.append(v[done:g0])
                pieces.append(jnp.where(sub == pick, val, v[g0:g0 + SUBLANES]))
                done = g0 + SUBLANES
            if done < v.shape[0]:
                pieces.append(v[done:])
            return jnp.concatenate(pieces, axis=0)

        prev = patched(prev, starts, 0)
        nxt = patched(nxt, ends, SUBLANES - 1)
        return prev * w_ref_[0:1, :] + a * w_ref_[1:2, :] + nxt * w_ref_[2:3, :] + b_ref_[...]

    def load_signal(r, _):
        r0 = pl.multiple_of(r * sr, sr)
        if short_signal:
            xs_ref[pl.ds(r0, sr), :] = short_conv(x_ref, sw_ref, sb_ref, r0)
        else:
            xs_ref[pl.ds(r0, sr), :] = x_ref[pl.ds(r0, sr), :].astype(F32)
        return 0

    lax.fori_loop(0, rows // sr, load_signal, 0)

    def fwd(j, _):
        xj = xs_ref[pl.ds(pl.multiple_of(j * t_blk, t_blk), t_blk), :]
        u_ref[j] = _dot(f_ref[...], xj.astype(BF16))
        return 0

    lax.fori_loop(0, n_blocks, fwd, 0)

    def out_block(sb, _):
        i = sb % nb
        first = sb - i

        def mac(r, _):
            re = pl.ds(pl.multiple_of(r * rc, SUBLANES), rc)
            im = pl.ds(pl.multiple_of(fpad + r * rc, SUBLANES), rc)
            are = jnp.zeros((rc, c), F32)
            aim = jnp.zeros((rc, c), F32)
            for j in range(nb):
                d = i - j + nb
                hre, him = h_ref[d, re, :], h_ref[d, im, :]
                ure, uim = u_ref[first + j, re, :], u_ref[first + j, im, :]
                are = are + hre * ure - him * uim
                aim = aim + hre * uim + him * ure
            y_ref[re, :] = are
            y_ref[im, :] = aim
            return 0

        lax.fori_loop(0, fpad // rc, mac, 0)
        blk = pl.ds(pl.multiple_of(sb * t_blk, t_blk), t_blk)
        conv = _dot(g_ref[...], y_ref[...].astype(BF16))
        xs_ref[blk, :] = conv + xs_ref[blk, :] * bias_ref[...]
        return 0

    lax.fori_loop(0, n_blocks, out_block, 0)

    def gate(r, _):
        r0 = pl.multiple_of(r * sr, sr)
        o_ref[pl.ds(r0, sr), :] = (xs_ref[pl.ds(r0, sr), :] * short_conv(gz_ref, gw_ref, gb_ref, r0)).astype(o_ref.dtype)
        return 0

    lax.fori_loop(0, rows // sr, gate, 0)


def _hyena_conv(x, x_col, zh, gate_col, short_w, short_b, bias, spectra, order, length, t_blk,
                dft_fwd, dft_inv, short_signal):
    c = bias.shape[0]
    n = zh.shape[0]
    nb = length // t_blk
    fpad = _fpad(t_blk)
    rc = max(r for r in (8, 24, 40) if fpad % r == 0)
    rows = min(max(length, HY_STEP_ROWS), n)
    big = _SINGLE if 2 * nb * 2 * fpad * c * 4 > HY_SINGLE_BUFFER_BYTES else None
    return pl.pallas_call(
        functools.partial(_hyena_conv_kernel, length=length, nb=nb, short_signal=short_signal, rc=rc),
        grid=(n // rows,),
        in_specs=[
            pl.BlockSpec((rows, c), lambda b: (b, x_col)),
            pl.BlockSpec((rows, c), lambda b: (b, gate_col)),
            pl.BlockSpec((3, c), lambda b: (0, x_col)),
            pl.BlockSpec((1, c), lambda b: (0, x_col)),
            pl.BlockSpec((3, c), lambda b: (0, gate_col)),
            pl.BlockSpec((1, c), lambda b: (0, gate_col)),
            pl.BlockSpec((1, c), lambda b: (0, 0)),
            pl.BlockSpec((2 * nb, 2 * fpad, c), lambda b: (0, 0, order), pipeline_mode=big),
            pl.BlockSpec((2 * fpad, t_blk), _const),
            pl.BlockSpec((t_blk, 2 * fpad), _const),
        ],
        out_specs=pl.BlockSpec((rows, c), lambda b: (b, 0)),
        out_shape=jax.ShapeDtypeStruct((n, c), BF16),
        scratch_shapes=[
            pltpu.VMEM((rows, c), F32),
            pltpu.VMEM((rows // t_blk, 2 * fpad, c), F32),
            pltpu.VMEM((2 * fpad, c), F32),
        ],
        compiler_params=_params("arbitrary"),
        name="hyena_conv",
    )(x, zh, short_w, short_b.reshape(1, -1), short_w, short_b.reshape(1, -1), bias.reshape(1, c), spectra,
      dft_fwd, dft_inv)


def _hyena(zh, short_w, short_b, hy_bias, filt, length):
    t_blk = min(HY_MAX_BLOCK, length)
    dft_fwd, dft_inv = _dft_tables(t_blk)
    spectra = _hyena_filters(length, t_blk, *filt, dft_fwd)
    u1 = _hyena_conv(zh, 0, zh, 1, short_w, short_b, hy_bias[0], spectra, 0, length, t_blk,
                     dft_fwd, dft_inv, True)
    return _hyena_conv(u1, 0, zh, 2, short_w, short_b, hy_bias[1], spectra, 1, length, t_blk,
                       dft_fwd, dft_inv, False)


def _mix_ffn_kernel(x_ref, xp_ref, xn_ref, a_ref, ap_ref, an_ref, c_ref, cp_ref, cn_ref, h_ref, hp_ref, hn_ref,
                    mod_ref, wo_ref, n2_ref, wup_ref, dww_ref, dwb_ref, wdn_ref, fn_ref,
                    o_ref, p_ref, act_ref, *, seq, final):
    i = pl.program_id(0)
    tm, d = x_ref.shape
    ffn = wdn_ref.shape[0]
    halo = FFN_HALO
    seg = min(seq, tm)
    nseg = tm // seg
    inside = seq > tm

    def mixed(refs, rows):
        return jnp.concatenate([r[rows, :] for r in refs], axis=1)

    mix_in = mixed((a_ref, c_ref, h_ref), slice(None))
    x_in = x_ref[...]
    if inside:
        lo, hi = slice(BF16_SUBLANES - halo, BF16_SUBLANES), slice(0, halo)
        mix_in = jnp.concatenate([mixed((ap_ref, cp_ref, hp_ref), lo), mix_in,
                                  mixed((an_ref, cn_ref, hn_ref), hi)], axis=0)
        x_in = jnp.concatenate([xp_ref[...], x_in, xn_ref[...]], axis=0)
    x1 = x_in + mod_ref[2:3, :] * _dot(mix_in, wo_ref[...])

    def norm_mod(x):
        ms = jnp.mean(x * x, axis=-1, keepdims=True)
        h = x * lax.rsqrt(ms + NORM_EPS) * n2_ref[...]
        return h * (1.0 + mod_ref[4:5, :]) + mod_ref[3:4, :]

    zeros = jnp.zeros((halo, d), F32)
    if inside:
        tiles = seq // tm
        hx = norm_mod(x1)
        before = jnp.where(i % tiles == 0, 0.0, hx[:halo])
        after = jnp.where(i % tiles == tiles - 1, 0.0, hx[halo + tm:])
        hm = hx[halo:halo + tm]
        x1 = x1[halo:halo + tm]
    else:
        before, after = zeros, zeros
        hm = norm_mod(x1)
    parts = [before]
    for s in range(nseg):
        parts += [hm[s * seg:(s + 1) * seg], after if s == nseg - 1 else zeros]
    hb = jnp.concatenate(parts, axis=0).astype(BF16)

    p_ref[0] = _dot(hb, wup_ref[:, :ffn])
    p_ref[1] = _dot(hb, wup_ref[:, ffn:])
    ext = p_ref.shape[1]

    def dwconv(which, cols):
        wcols = slice(which * ffn + cols.start, which * ffn + cols.stop)
        w = dww_ref[:, wcols]
        p = p_ref[which, :, cols]
        prev = pltpu.roll(p, 1, axis=0)
        nxt = pltpu.roll(p, ext - 1, axis=0)
        out = []
        for s in range(nseg):
            o = halo + s * (seg + halo)
            out.append(prev[o:o + seg, :] * w[0:1] + p[o:o + seg, :] * w[1:2] + nxt[o:o + seg, :] * w[2:3]
                       + dwb_ref[:, wcols])
        return out[0] if nseg == 1 else jnp.concatenate(out, axis=0)

    for c0 in range(0, ffn, FFN_STRIP):
        cols = slice(c0, c0 + FFN_STRIP)
        gate = dwconv(1, cols)
        act_ref[:, cols] = (gate * _sigmoid(gate) * dwconv(0, cols)).astype(BF16)

    out = x1 + mod_ref[5:6, :] * _dot(act_ref[...], wdn_ref[...])
    if final:
        ms = jnp.mean(out * out, axis=-1, keepdims=True)
        out = out * lax.rsqrt(ms + NORM_EPS) * fn_ref[...]
    o_ref[...] = out


def _mix_ffn(x, attn, conv, hy, mod_l, cond_base, w_o, norm2, w_up, dw_w, dw_b, w_down, final_norm, seq, final):
    n, d = x.shape
    tm = TOKEN_TILE
    ffn = w_down.shape[0]
    assert ffn % FFN_STRIP == 0

    def tile_and_halos(arr, halo):
        w = arr.shape[1]
        per_tile = tm // halo
        last = n // halo - 1
        return [pl.BlockSpec((tm, w), lambda i: (i, 0)),
                pl.BlockSpec((halo, w), lambda i: (jnp.maximum(i * per_tile - 1, 0), 0)),
                pl.BlockSpec((halo, w), lambda i: (jnp.minimum((i + 1) * per_tile, last), 0))]

    return pl.pallas_call(
        functools.partial(_mix_ffn_kernel, seq=seq, final=final),
        grid=(n // tm,),
        in_specs=[
            *tile_and_halos(x, FFN_HALO),
            *tile_and_halos(attn, BF16_SUBLANES),
            *tile_and_halos(conv, BF16_SUBLANES),
            *tile_and_halos(hy, BF16_SUBLANES),
            _mod_spec(d, tm, seq, cond_base),
            pl.BlockSpec(w_o.shape, _const, pipeline_mode=_SINGLE),
            pl.BlockSpec((1, d), _const),
            pl.BlockSpec(w_up.shape, _const, pipeline_mode=_SINGLE),
            pl.BlockSpec((3, 2 * ffn), _const),
            pl.BlockSpec((1, 2 * ffn), _const),
            pl.BlockSpec(w_down.shape, _const, pipeline_mode=_SINGLE),
            pl.BlockSpec((1, d), _const),
        ],
        out_specs=pl.BlockSpec((tm, d), lambda i: (i, 0)),
        out_shape=jax.ShapeDtypeStruct((n, d), F32),
        scratch_shapes=[
            pltpu.VMEM((2, tm + (tm // min(seq, tm) + 1) * FFN_HALO, ffn), F32),
            pltpu.VMEM((tm, ffn), BF16),
        ],
        compiler_params=_params("parallel"),
        name="mix_ffn",
    )(x, x, x, attn, attn, attn, conv, conv, conv, hy, hy, hy, mod_l, w_o, norm2.reshape(1, d), w_up, dw_w,
      dw_b.reshape(1, -1), w_down, final_norm.reshape(1, d))


def kernel(x_prompt, x_sample, cache_k, cache_v, c, c_ctx, norm1, norm2, w_mod, b_mod, w_in, q_norm, k_norm,
           conv_dw_w, conv_dw_b, conv_gn_w, conv_gn_b, conv_pw_w, conv_pw_b, hy_short_w, hy_short_b,
           hy_f_w1, hy_f_b1, hy_f_w2, hy_f_b2, hy_f_w3, hy_f_b3, hy_freq, hy_bias, w_o, ffn_w_up,
           ffn_dw_w, ffn_dw_b, ffn_w_down, final_norm):
    batch, ctx_seq, d = x_prompt.shape
    dec_batch, lat_seq, _ = x_sample.shape
    past = cache_k.shape[2]
    depth = norm1.shape[0]
    conv_w = conv_pw_w.shape[1]
    hy_w = hy_bias.shape[2]
    assert 1 + dec_batch <= SUBLANES

    cond = jnp.concatenate([c_ctx[None, :], c, jnp.zeros((SUBLANES - 1 - dec_batch, d), F32)], axis=0)
    mod = _modulation(cond, w_mod, b_mod).transpose(0, 2, 1, 3)

    xs = [x_prompt.reshape(batch * ctx_seq, d), x_sample.reshape(dec_batch * lat_seq, d)]
    seqs = [ctx_seq, lat_seq]
    cond_bases = [0, 1]
    new_k, new_v = [], []
    for l in range(depth):
        w_in_l = w_in[l].astype(BF16)
        w_o_l = w_o[l].astype(BF16)
        w_up_l = ffn_w_up[l].astype(BF16)
        w_down_l = ffn_w_down[l].astype(BF16)
        filt = (hy_f_w1[l], hy_f_b1[l], hy_f_w2[l], hy_f_b2[l], hy_f_w3[l], hy_f_b3[l], hy_freq[l])
        for path in range(2):
            x, seq, base = xs[path], seqs[path], cond_bases[path]
            qt, k, kb, v, vt, zc, zh = _inproj(x, mod[l], base, norm1[l], w_in_l, q_norm[l], k_norm[l],
                                               seq, path == 1, 2 * conv_w, (HYENA_ORDER + 1) * hy_w)
            if path == 0:
                new_k.append(k.reshape(batch, ctx_seq, N_KV_HEADS, HEAD_DIM))
                new_v.append(v.reshape(batch, ctx_seq, N_KV_HEADS, HEAD_DIM))
                cache = None
            else:
                cache = (cache_k[:, l].reshape(dec_batch, past, KV_WIDTH).astype(BF16),
                         cache_v[:, l].reshape(dec_batch, past, KV_WIDTH).transpose(0, 2, 1).astype(BF16))
            attn = _attention(qt, kb, vt, seq, cache)
            conv = _conformer(zc, conv_dw_w[l], conv_dw_b[l], conv_gn_w[l], conv_gn_b[l], conv_pw_w[l],
                              conv_pw_b[l], seq)
            hy = _hyena(zh, hy_short_w[l], hy_short_b[l], hy_bias[l], filt, seq)
            xs[path] = _mix_ffn(x, attn, conv, hy, mod[l], base, w_o_l, norm2[l], w_up_l, ffn_dw_w[l],
                                ffn_dw_b[l], w_down_l, final_norm, seq, l == depth - 1)

    y_prompt = xs[0].reshape(batch, ctx_seq, d)
    y_sample = xs[1].reshape(dec_batch, lat_seq, d)
    return y_prompt, y_sample, jnp.stack(new_k, axis=1), jnp.stack(new_v, axis=1)
```

```python
import functools
import math

import numpy as np
import jax
import jax.numpy as jnp
from jax import lax
from jax.experimental import pallas as pl
from jax.experimental.pallas import tpu as pltpu

F32 = jnp.float32
BF16 = jnp.bfloat16

GRID_W = 64
N_HEADS = 8
N_KV_HEADS = 2
HEAD_DIM = 64
GROUP = N_HEADS // N_KV_HEADS
ATTN_WIDTH = N_HEADS * HEAD_DIM
KV_WIDTH = N_KV_HEADS * HEAD_DIM
CONV_GROUP_WIDTH = 64
CONV_KSIZE = 31
HYENA_ORDER = 2
HYENA_POS_BANDS = 16
HYENA_MIN_DECAY = math.log(1e-2) / 1.5
HYENA_MAX_DECAY = math.log(1e-2) / 0.3
ROPE_BASE = 10000.0
NORM_EPS = 1e-6

LANES = 128
SUBLANES = 8
BF16_SUBLANES = 16
VMEM_LIMIT = 56 * 1024 * 1024

TOKEN_TILE = 512
INPROJ_SLAB = 256
CONV_TILE = 512
CONV_HALO = 16
ATTN_Q_TILE = 1024
ATTN_K_CHUNK = 512
HY_MAX_BLOCK = 512
HY_SHORT_ROWS = 512
HY_STEP_ROWS = 2048
HY_SINGLE_BUFFER_BYTES = 8 * 1024 * 1024
FFN_HALO = 8
FFN_STRIP = 256

_SINGLE = pl.Buffered(1)


def _params(*sem):
    return pltpu.CompilerParams(dimension_semantics=sem, vmem_limit_bytes=VMEM_LIMIT)


def _split(x):
    hi = x.astype(BF16)
    lo = (x - hi.astype(F32)).astype(BF16)
    return hi, lo


def _dot(a, b):
    return jnp.dot(a, b, preferred_element_type=F32)


def _dot3(a, b_hi, b_lo):
    a_hi, a_lo = _split(a)
    return _dot(a_hi, b_hi) + _dot(a_hi, b_lo) + _dot(a_lo, b_hi)


def _dot3l(a_hi, a_lo, b):
    b_hi, b_lo = _split(b)
    return _dot(a_hi, b_hi) + _dot(a_lo, b_hi) + _dot(a_hi, b_lo)


def _sigmoid(x):
    return 1.0 / (1.0 + jnp.exp(-x))


def _const(*_):
    return (0, 0)


def _mod_kernel(cond_ref, w_ref, b_ref, o_ref):
    c = cond_ref[...]
    s = c * _sigmoid(c)
    o_ref[...] = _dot(s.astype(BF16), w_ref[...].astype(BF16)) + b_ref[...]


def _modulation(cond, w_mod, b_mod):
    depth, d, _ = w_mod.shape
    return pl.pallas_call(
        _mod_kernel,
        grid=(depth, 6),
        in_specs=[
            pl.BlockSpec((SUBLANES, d), lambda l, j: (0, 0)),
            pl.BlockSpec((None, d, d), lambda l, j: (l, 0, j)),
            pl.BlockSpec((None, 1, d), lambda l, j: (l, 0, j)),
        ],
        out_specs=pl.BlockSpec((None, None, SUBLANES, d), lambda l, j: (l, j, 0, 0)),
        out_shape=jax.ShapeDtypeStruct((depth, 6, SUBLANES, d), F32),
        compiler_params=_params("arbitrary", "arbitrary"),
        name="modulation",
    )(cond, w_mod, b_mod.reshape(depth, 1, 6 * d))


def _mod_spec(d, tile, seq, cond_base):
    per_seq = seq // tile if seq >= tile else 0
    if cond_base == 0:
        return pl.BlockSpec((None, 6, d), lambda i: (0, 0, 0))
    return pl.BlockSpec((None, 6, d), lambda i: (cond_base + i // per_seq, 0, 0))


def _inproj_kernel(x_ref, mod_ref, n1_ref, w_ref, qn_ref, kn_ref, g_ref, cos_ref, sin_ref,
                   qt_ref, k_ref, kb_ref, v_ref, vt_ref, zc_ref, zh_ref, *, rotary):
    x = x_ref[...]
    ms = jnp.mean(x * x, axis=-1, keepdims=True)
    h = x * lax.rsqrt(ms + NORM_EPS) * n1_ref[...]
    h = h * (1.0 + mod_ref[1:2, :]) + mod_ref[0:1, :]
    proj = _dot(h.astype(BF16), w_ref[...])

    def head_norm(t, w_ref_):
        width = t.shape[1]
        msq = _dot((t * t).astype(BF16), g_ref[:width, :width])
        t = t * lax.rsqrt(msq + NORM_EPS) * w_ref_[:, :width]
        if rotary:
            lane = lax.broadcasted_iota(jnp.int32, (1, width), 1)
            partner = jnp.where((lane % 32) < 16, pltpu.roll(t, width - 16, axis=1), pltpu.roll(t, 16, axis=1))
            t = t * cos_ref[:, :width] + partner * sin_ref[:, :width]
        return t

    scale = HEAD_DIM ** -0.5 * math.log2(math.e)
    slab = g_ref.shape[0]
    for c in range(ATTN_WIDTH // slab):
        sl = slice(c * slab, (c + 1) * slab)
        qn = head_norm(proj[:, sl], qn_ref) * scale
        qt_ref[sl, :] = qn.T.astype(qt_ref.dtype)

    kn = head_norm(proj[:, ATTN_WIDTH:ATTN_WIDTH + KV_WIDTH], kn_ref)
    k_ref[...] = kn
    kb_ref[...] = kn.astype(kb_ref.dtype)
    o = ATTN_WIDTH + KV_WIDTH
    v = proj[:, o:o + KV_WIDTH]
    v_ref[...] = v
    vt_ref[...] = v.T.astype(vt_ref.dtype)
    o += KV_WIDTH
    zc_ref[...] = proj[:, o:o + zc_ref.shape[1]].astype(zc_ref.dtype)
    o += zc_ref.shape[1]
    zh_ref[...] = proj[:, o:o + zh_ref.shape[1]].astype(zh_ref.dtype)


def _rope_tables(length, width):
    pos = np.arange(length)
    rows, cols = pos // GRID_W, pos % GRID_W
    half = HEAD_DIM // 2
    inv = ROPE_BASE ** (-np.arange(0, half, 2, dtype=np.float64) / half)
    d = np.arange(HEAD_DIM)
    p = np.where(d[None, :] < half, rows[:, None], cols[:, None]).astype(np.float64)
    ang = p * inv[d % (half // 2)][None, :]
    sign = np.where((d % half) < half // 2, -1.0, 1.0)[None, :]
    cos = np.tile(np.cos(ang), (1, width // HEAD_DIM))
    sin = np.tile(np.sin(ang) * sign, (1, width // HEAD_DIM))
    return jnp.asarray(cos, F32), jnp.asarray(sin, F32)


def _group_mean_matrix(width, group):
    idx = np.arange(width) // group
    return jnp.asarray((idx[:, None] == idx[None, :]).astype(np.float32) / group, BF16)


def _inproj(x, mod_l, cond_base, norm1, w_in, q_norm, k_norm, seq, rotary, zc_w, zh_w):
    n, d = x.shape
    tm = TOKEN_TILE
    per_seq = max(seq // tm, 1)
    slab = INPROJ_SLAB
    cos, sin = _rope_tables(seq if rotary else tm, slab)
    g = _group_mean_matrix(slab, HEAD_DIM)
    tab = (lambda i: (i % per_seq, 0)) if rotary else _const
    row = lambda i: (i, 0)
    col = lambda i: (0, i)
    return pl.pallas_call(
        functools.partial(_inproj_kernel, rotary=rotary),
        grid=(n // tm,),
        in_specs=[
            pl.BlockSpec((tm, d), row),
            _mod_spec(d, tm, seq, cond_base),
            pl.BlockSpec((1, d), _const),
            pl.BlockSpec(w_in.shape, _const),
            pl.BlockSpec((1, slab), _const),
            pl.BlockSpec((1, slab), _const),
            pl.BlockSpec((slab, slab), _const),
            pl.BlockSpec((tm, slab), tab),
            pl.BlockSpec((tm, slab), tab),
        ],
        out_specs=[
            pl.BlockSpec((ATTN_WIDTH, tm), col),
            pl.BlockSpec((tm, KV_WIDTH), row),
            pl.BlockSpec((tm, KV_WIDTH), row),
            pl.BlockSpec((tm, KV_WIDTH), row),
            pl.BlockSpec((KV_WIDTH, tm), col),
            pl.BlockSpec((tm, zc_w), row),
            pl.BlockSpec((tm, zh_w), row),
        ],
        out_shape=[
            jax.ShapeDtypeStruct((ATTN_WIDTH, n), BF16),
            jax.ShapeDtypeStruct((n, KV_WIDTH), F32),
            jax.ShapeDtypeStruct((n, KV_WIDTH), BF16),
            jax.ShapeDtypeStruct((n, KV_WIDTH), F32),
            jax.ShapeDtypeStruct((KV_WIDTH, n), BF16),
            jax.ShapeDtypeStruct((n, zc_w), BF16),
            jax.ShapeDtypeStruct((n, zh_w), BF16),
        ],
        compiler_params=_params("parallel"),
        name="inproj",
    )(x, mod_l, norm1.reshape(1, d), w_in,
      jnp.tile(q_norm, slab // HEAD_DIM).reshape(1, slab),
      jnp.tile(k_norm, slab // HEAD_DIM).reshape(1, slab), g, cos, sin)


def _attn_kernel(*refs, chunks, cache_chunks):
    if cache_chunks:
        qt_ref, k_ref, vt_ref, ck_ref, cvt_ref, o_ref = refs
    else:
        qt_ref, k_ref, vt_ref, o_ref = refs
    tq = qt_ref.shape[1]
    rows = GROUP * HEAD_DIM
    heads = qt_ref.shape[0] // rows

    def ones_rows(size):
        r = lax.broadcasted_iota(jnp.int32, (BF16_SUBLANES, size), 0)
        return jnp.where(r == 0, 1.0, 0.0).astype(BF16)

    def update(s, v, m, acc):
        m_new = jnp.maximum(m, jnp.max(s, axis=0, keepdims=True))
        alpha = jnp.exp2(m - m_new)
        p = jnp.exp2(s - m_new).astype(BF16)
        v_aug = jnp.concatenate([v, ones_rows(v.shape[1])], axis=0)
        return m_new, acc * alpha + _dot(v_aug, p)

    for hh in range(heads):
        h = hh if heads == N_KV_HEADS else pl.program_id(1)
        q4 = jnp.concatenate([qt_ref[hh * rows + g * HEAD_DIM:hh * rows + (g + 1) * HEAD_DIM, :]
                              for g in range(GROUP)], axis=1)
        zero = jnp.zeros_like(q4)
        q_pad = jnp.where(h == 0, jnp.concatenate([q4, zero], axis=0), jnp.concatenate([zero, q4], axis=0))
        vrows = slice(hh * HEAD_DIM, (hh + 1) * HEAD_DIM)

        m = jnp.full((1, GROUP * tq), -1e30, F32)
        acc = jnp.zeros((HEAD_DIM + BF16_SUBLANES, GROUP * tq), F32)
        for start, size in chunks:
            m, acc = update(_dot(k_ref[start:start + size, :], q_pad), vt_ref[vrows, start:start + size], m, acc)
        for start, size in cache_chunks:
            m, acc = update(_dot(ck_ref[start:start + size, :], q_pad), cvt_ref[vrows, start:start + size], m, acc)
        out = acc[:HEAD_DIM] / acc[HEAD_DIM:HEAD_DIM + 1]
        out = jnp.concatenate([out[:, g * tq:(g + 1) * tq] for g in range(GROUP)], axis=0)
        o_ref[:, hh * rows:(hh + 1) * rows] = out.T.astype(o_ref.dtype)


def _chunks(n):
    return [(s, min(ATTN_K_CHUNK, n - s)) for s in range(0, n, ATTN_K_CHUNK)]


def _attention(qt, kb, vt, seq, cache=None):
    n = kb.shape[0]
    nb = n // seq
    tq = min(ATTN_Q_TILE, seq)
    nq = seq // tq
    hps = N_KV_HEADS if seq <= ATTN_K_CHUNK else 1
    rows = hps * GROUP * HEAD_DIM
    in_specs = [
        pl.BlockSpec((rows, tq), lambda b, h, i: (h, b * nq + i)),
        pl.BlockSpec((seq, KV_WIDTH), lambda b, h, i: (b, 0)),
        pl.BlockSpec((hps * HEAD_DIM, seq), lambda b, h, i: (h, b)),
    ]
    args = [qt, kb, vt]
    cache_chunks = []
    if cache is not None:
        past = cache[0].shape[1]
        in_specs += [
            pl.BlockSpec((None, past, KV_WIDTH), lambda b, h, i: (b, 0, 0)),
            pl.BlockSpec((None, hps * HEAD_DIM, past), lambda b, h, i: (b, h, 0)),
        ]
        args += list(cache)
        cache_chunks = _chunks(past)
    return pl.pallas_call(
        functools.partial(_attn_kernel, chunks=_chunks(seq), cache_chunks=cache_chunks),
        grid=(nb, N_KV_HEADS // hps, nq),
        in_specs=in_specs,
        out_specs=pl.BlockSpec((tq, rows), lambda b, h, i: (b * nq + i, h)),
        out_shape=jax.ShapeDtypeStruct((n, ATTN_WIDTH), BF16),
        compiler_params=_params("parallel", "parallel", "parallel"),
        name="attention",
    )(*args)


def _conformer_kernel(z_ref, zp_ref, zn_ref, dww_ref, dwb_ref, gnw_ref, gnb_ref, g_ref, pww_ref, pwb_ref,
                      o_ref, ubuf_ref, cbuf_ref, *, tiles_per_seq):
    i = pl.program_id(0)
    tt, c = o_ref.shape
    halo = CONV_HALO
    first = (i % tiles_per_seq) == 0
    last = (i % tiles_per_seq) == tiles_per_seq - 1

    def glu(ref):
        z = ref[...].astype(F32)
        return z[:, :c] * _sigmoid(z[:, c:])

    ubuf_ref[0, 0:halo, :] = jnp.where(first, 0.0, glu(zp_ref))
    ubuf_ref[0, halo:halo + tt, :] = glu(z_ref)
    ubuf_ref[0, halo + tt:, :] = jnp.where(last, 0.0, glu(zn_ref))
    span = tt + 2 * halo - SUBLANES
    for s in range(1, SUBLANES):
        ubuf_ref[s, 0:span, :] = ubuf_ref[0, s:s + span, :]

    pad = (CONV_KSIZE - 1) // 2
    rows = 64
    for r in range(tt // rows):
        acc = jnp.zeros((rows, c), F32) + dwb_ref[...]
        for k in range(CONV_KSIZE):
            off = halo + r * rows - pad + k
            s = off % SUBLANES
            acc = acc + ubuf_ref[s, off - s:off - s + rows, :] * dww_ref[k:k + 1, :]
        cbuf_ref[r * rows:(r + 1) * rows, :] = acc

    g = g_ref[...]
    u = cbuf_ref[...]
    hi, lo = _split(u)
    mu = _dot(hi, g) + _dot(lo, g)
    dlt = u - mu
    var = _dot((dlt * dlt).astype(BF16), g)
    un = dlt * lax.rsqrt(var + NORM_EPS) * gnw_ref[...] + gnb_ref[...]
    act = un * _sigmoid(un)
    o_ref[...] = (_dot(act.astype(BF16), pww_ref[...]) + pwb_ref[...]).astype(o_ref.dtype)


def _conformer(zc, dw_w, dw_b, gn_w, gn_b, pw_w, pw_b, seq):
    n, c2 = zc.shape
    c = c2 // 2
    tt = min(CONV_TILE, seq)
    hb = tt // CONV_HALO
    n_halo_blocks = n // CONV_HALO
    dw_w = jnp.concatenate([dw_w, jnp.zeros((1, c), F32)], axis=0)
    g = _group_mean_matrix(c, CONV_GROUP_WIDTH)
    return pl.pallas_call(
        functools.partial(_conformer_kernel, tiles_per_seq=seq // tt),
        grid=(n // tt,),
        in_specs=[
            pl.BlockSpec((tt, c2), lambda i: (i, 0)),
            pl.BlockSpec((CONV_HALO, c2), lambda i: (jnp.maximum(i * hb - 1, 0), 0)),
            pl.BlockSpec((CONV_HALO, c2), lambda i: (jnp.minimum((i + 1) * hb, n_halo_blocks - 1), 0)),
            pl.BlockSpec((CONV_KSIZE + 1, c), _const),
            pl.BlockSpec((1, c), _const),
            pl.BlockSpec((1, c), _const),
            pl.BlockSpec((1, c), _const),
            pl.BlockSpec((c, c), _const),
            pl.BlockSpec((c, c), _const),
            pl.BlockSpec((1, c), _const),
        ],
        out_specs=pl.BlockSpec((tt, c), lambda i: (i, 0)),
        out_shape=jax.ShapeDtypeStruct((n, c), BF16),
        scratch_shapes=[pltpu.VMEM((SUBLANES, tt + 2 * CONV_HALO, c), F32), pltpu.VMEM((tt, c), F32)],
        compiler_params=_params("parallel"),
        name="conformer",
    )(zc, zc, zc, dw_w, dw_b.reshape(1, c), gn_w.reshape(1, c), gn_b.reshape(1, c), g,
      pw_w.astype(BF16), pw_b.reshape(1, c))


def _fpad(t):
    return -(-(t + 1) // SUBLANES) * SUBLANES


def _dft_tables(t_blk):
    fpad = _fpad(t_blk)
    t = np.arange(t_blk)
    k = np.arange(fpad)
    ang = ((k[:, None] * t[None, :]) % (2 * t_blk)) * (2.0 * np.pi / (2 * t_blk))
    live = (k <= t_blk)[:, None]
    fwd = np.concatenate([np.where(live, np.cos(ang), 0.0), np.where(live, -np.sin(ang), 0.0)], axis=0)
    ck = np.where((k == 0) | (k == t_blk), 1.0, 2.0)[:, None] / (2 * t_blk)
    inv = np.concatenate([np.where(live, ck * np.cos(ang), 0.0), np.where(live, -ck * np.sin(ang), 0.0)], axis=0).T

    return jnp.asarray(fwd, BF16), jnp.asarray(inv, BF16)


def _filter_features(length):
    lag = np.arange(2 * length) - length
    idx = np.minimum(np.abs(lag), length - 1)
    t = np.linspace(0.0, 1.0, length)[idx]
    w_ang = 2.0 * np.pi * idx / length
    bands = np.linspace(1e-4, HYENA_POS_BANDS - 1, HYENA_POS_BANDS)
    z = np.zeros((2 * length, LANES), np.float64)
    z[:, 0] = t
    z[:, 1:1 + HYENA_POS_BANDS] = np.cos(bands[None, :] * w_ang[:, None])
    z[:, 1 + HYENA_POS_BANDS:1 + 2 * HYENA_POS_BANDS] = np.sin(-bands[None, :] * w_ang[:, None])
    z[:, 1 + 2 * HYENA_POS_BANDS] = (lag > -length).astype(np.float64)
    return jnp.asarray(z, F32), jnp.asarray(z.T, F32)


def _hyena_filter_kernel(z_ref, zt_ref, w1t_ref, b1_ref, w2t_ref, b2_ref, fr_ref, w3_ref, b3_ref, dl_ref,
                         fh_ref, h_ref, prev_ref):
    s = pl.program_id(0)
    fpad = prev_ref.shape[0] // 2
    z = z_ref[...]
    fr = fr_ref[...]

    def dense_t(wt_ref, a, b_ref_):
        w_hi, w_lo = _split(wt_ref[...])
        return _dot3l(w_hi, w_lo, a) + b_ref_[...]

    hid = jnp.sin(fr * dense_t(w1t_ref, zt_ref[...], b1_ref))
    hid = jnp.sin(fr * dense_t(w2t_ref, hid, b2_ref)).T
    w_hi, w_lo = _split(w3_ref[...])
    h = _dot3(hid, w_hi, w_lo) + b3_ref[...]
    t = z[:, 0:1]
    live = z[:, 1 + 2 * HYENA_POS_BANDS:2 + 2 * HYENA_POS_BANDS]
    h = h * jnp.exp(-t * dl_ref[...]) * live
    spec = _dot(fh_ref[...], h.astype(BF16))

    @pl.when(s == 0)
    def _():
        prev_ref[...] = jnp.zeros_like(prev_ref)

    row = lax.broadcasted_iota(jnp.int32, (2 * fpad, 1), 0)
    sign = (1 - 2 * ((row % fpad) % 2)).astype(F32)
    h_ref[...] = spec + sign * prev_ref[...]
    prev_ref[...] = spec


def _hyena_filters(length, t_blk, w1, b1, w2, b2, w3, b3, freq, dft_fwd):
    nb = length // t_blk
    fpad = _fpad(t_blk)
    fh = w2.shape[0]
    oc = w3.shape[1] // 2
    c = oc // HYENA_ORDER
    z, zt = _filter_features(length)
    w1t = jnp.concatenate([w1, jnp.zeros((LANES - w1.shape[0], fh), F32)], axis=0).T
    w3d = w3.reshape(fh, HYENA_ORDER, 2, c).transpose(2, 0, 1, 3).reshape(2, fh, oc)
    b3d = b3.reshape(HYENA_ORDER, 2, c).transpose(1, 0, 2).reshape(2, 1, oc)
    deltas = np.abs(np.linspace(HYENA_MIN_DECAY, HYENA_MAX_DECAY, c))
    dl = jnp.asarray(np.tile(deltas, HYENA_ORDER)[None, :], F32)
    direction = lambda s: (jnp.where(s >= nb, 0, 1), 0, 0)
    return pl.pallas_call(
        _hyena_filter_kernel,
        grid=(2 * nb,),
        in_specs=[
            pl.BlockSpec((t_blk, LANES), lambda s: (s, 0)),
            pl.BlockSpec((LANES, t_blk), lambda s: (0, s)),
            pl.BlockSpec((fh, LANES), _const),
            pl.BlockSpec((fh, 1), _const),
            pl.BlockSpec((fh, fh), _const),
            pl.BlockSpec((fh, 1), _const),
            pl.BlockSpec((fh, 1), _const),
            pl.BlockSpec((None, fh, oc), direction),
            pl.BlockSpec((None, 1, oc), direction),
            pl.BlockSpec((1, oc), _const),
            pl.BlockSpec((2 * fpad, t_blk), _const),
        ],
        out_specs=pl.BlockSpec((None, 2 * fpad, oc), lambda s: (s, 0, 0)),
        out_shape=jax.ShapeDtypeStruct((2 * nb, 2 * fpad, oc), F32),
        scratch_shapes=[pltpu.VMEM((2 * fpad, oc), F32)],
        compiler_params=_params("arbitrary"),
        name="hyena_filter",
    )(z, zt, w1t, b1.reshape(fh, 1), w2.T, b2.reshape(fh, 1), freq.reshape(fh, 1), w3d, b3d, dl,
      dft_fwd)


def _hyena_conv_kernel(x_ref, gz_ref, sw_ref, sb_ref, gw_ref, gb_ref, bias_ref, h_ref, f_ref, g_ref,
                       o_ref, xs_ref, u_ref, y_ref, *, length, nb, short_signal, rc):
    rows, c = x_ref.shape
    t_blk = length // nb
    n_blocks = rows // t_blk
    fpad = y_ref.shape[0] // 2
    sr = min(HY_SHORT_ROWS, rows)

    def short_conv(ref, w_ref_, b_ref_, r0):
        a = ref[pl.ds(r0, sr), :].astype(F32)
        w = BF16_SUBLANES
        before = ref[pl.ds(pl.multiple_of(jnp.maximum(r0 - w, 0), w), w), :].astype(F32)[w - 1:, :]
        after = ref[pl.ds(pl.multiple_of(jnp.minimum(r0 + sr, rows - w), w), w), :].astype(F32)[:1, :]
        prev = pltpu.roll(a, 1, axis=0)
        nxt = pltpu.roll(a, sr - 1, axis=0)
        if length >= sr:
            before = jnp.where(r0 % length == 0, 0.0, before)
            after = jnp.where((r0 + sr) % length == 0, 0.0, after)
            starts = {0: before}
            ends = {sr: after}
        else:
            starts = {s: 0.0 for s in range(0, sr, length)}
            ends = {s + length: 0.0 for s in range(0, sr, length)}
        sub = lax.broadcasted_iota(jnp.int32, (SUBLANES, 1), 0)

        def patched(v, at, pick):
            pieces, done = [], 0
            for r, val in sorted(at.items()):
                g0 = (r if pick == 0 else r - SUBLANES)
                if g0 > done:
                    pieces.append(v[done:g0])
                pieces.append(jnp.where(sub == pick, val, v[g0:g0 + SUBLANES]))
                done = g0 + SUBLANES
            if done < v.shape[0]:
                pieces.append(v[done:])
            return jnp.concatenate(pieces, axis=0)

        prev = patched(prev, starts, 0)
        nxt = patched(nxt, ends, SUBLANES - 1)
        return prev * w_ref_[0:1, :] + a * w_ref_[1:2, :] + nxt * w_ref_[2:3, :] + b_ref_[...]

    def load_signal(r, _):
        r0 = pl.multiple_of(r * sr, sr)
        if short_signal:
            xs_ref[pl.ds(r0, sr), :] = short_conv(x_ref, sw_ref, sb_ref, r0)
        else:
            xs_ref[pl.ds(r0, sr), :] = x_ref[pl.ds(r0, sr), :].astype(F32)
        return 0

    lax.fori_loop(0, rows // sr, load_signal, 0)

    def fwd(j, _):
        xj = xs_ref[pl.ds(pl.multiple_of(j * t_blk, t_blk), t_blk), :]
        u_ref[j] = _dot(f_ref[...], xj.astype(BF16))
        return 0

    lax.fori_loop(0, n_blocks, fwd, 0)

    def out_block(sb, _):
        i = sb % nb
        first = sb - i

        def mac(r, _):
            re = pl.ds(pl.multiple_of(r * rc, SUBLANES), rc)
            im = pl.ds(pl.multiple_of(fpad + r * rc, SUBLANES), rc)
            are = jnp.zeros((rc, c), F32)
            aim = jnp.zeros((rc, c), F32)
            for j in range(nb):
                d = i - j + nb
                hre, him = h_ref[d, re, :], h_ref[d, im, :]
                ure, uim = u_ref[first + j, re, :], u_ref[first + j, im, :]
                are = are + hre * ure - him * uim
                aim = aim + hre * uim + him * ure
            y_ref[re, :] = are
            y_ref[im, :] = aim
            return 0

        lax.fori_loop(0, fpad // rc, mac, 0)
        blk = pl.ds(pl.multiple_of(sb * t_blk, t_blk), t_blk)
        conv = _dot(g_ref[...], y_ref[...].astype(BF16))
        xs_ref[blk, :] = conv + xs_ref[blk, :] * bias_ref[...]
        return 0

    lax.fori_loop(0, n_blocks, out_block, 0)

    def gate(r, _):
        r0 = pl.multiple_of(r * sr, sr)
        o_ref[pl.ds(r0, sr), :] = (xs_ref[pl.ds(r0, sr), :] * short_conv(gz_ref, gw_ref, gb_ref, r0)).astype(o_ref.dtype)
        return 0

    lax.fori_loop(0, rows // sr, gate, 0)


def _hyena_conv(x, x_col, zh, gate_col, short_w, short_b, bias, spectra, order, length, t_blk,
                dft_fwd, dft_inv, short_signal):
    c = bias.shape[0]
    n = zh.shape[0]
    nb = length // t_blk
    fpad = _fpad(t_blk)
    rc = max(r for r in (8, 24, 40) if fpad % r == 0)
    rows = min(max(length, HY_STEP_ROWS), n)
    big = _SINGLE if 2 * nb * 2 * fpad * c * 4 > HY_SINGLE_BUFFER_BYTES else None
    return pl.pallas_call(
        functools.partial(_hyena_conv_kernel, length=length, nb=nb, short_signal=short_signal, rc=rc),
        grid=(n // rows,),
        in_specs=[
            pl.BlockSpec((rows, c), lambda b: (b, x_col)),
            pl.BlockSpec((rows, c), lambda b: (b, gate_col)),
            pl.BlockSpec((3, c), lambda b: (0, x_col)),
            pl.BlockSpec((1, c), lambda b: (0, x_col)),
            pl.BlockSpec((3, c), lambda b: (0, gate_col)),
            pl.BlockSpec((1, c), lambda b: (0, gate_col)),
            pl.BlockSpec((1, c), lambda b: (0, 0)),
            pl.BlockSpec((2 * nb, 2 * fpad, c), lambda b: (0, 0, order), pipeline_mode=big),
            pl.BlockSpec((2 * fpad, t_blk), _const),
            pl.BlockSpec((t_blk, 2 * fpad), _const),
        ],
        out_specs=pl.BlockSpec((rows, c), lambda b: (b, 0)),
        out_shape=jax.ShapeDtypeStruct((n, c), BF16),
        scratch_shapes=[
            pltpu.VMEM((rows, c), F32),
            pltpu.VMEM((rows // t_blk, 2 * fpad, c), F32),
            pltpu.VMEM((2 * fpad, c), F32),
        ],
        compiler_params=_params("arbitrary"),
        name="hyena_conv",
    )(x, zh, short_w, short_b.reshape(1, -1), short_w, short_b.reshape(1, -1), bias.reshape(1, c), spectra,
      dft_fwd, dft_inv)


def _hyena(zh, short_w, short_b, hy_bias, filt, length):
    t_blk = min(HY_MAX_BLOCK, length)
    dft_fwd, dft_inv = _dft_tables(t_blk)
    spectra = _hyena_filters(length, t_blk, *filt, dft_fwd)
    u1 = _hyena_conv(zh, 0, zh, 1, short_w, short_b, hy_bias[0], spectra, 0, length, t_blk,
                     dft_fwd, dft_inv, True)
    return _hyena_conv(u1, 0, zh, 2, short_w, short_b, hy_bias[1], spectra, 1, length, t_blk,
                       dft_fwd, dft_inv, False)


def _mix_ffn_kernel(x_ref, xp_ref, xn_ref, a_ref, ap_ref, an_ref, c_ref, cp_ref, cn_ref, h_ref, hp_ref, hn_ref,
                    mod_ref, wo_ref, n2_ref, wup_ref, dww_ref, dwb_ref, wdn_ref, fn_ref,
                    o_ref, p_ref, act_ref, *, seq, final):
    i = pl.program_id(0)
    tm, d = x_ref.shape
    ffn = wdn_ref.shape[0]
    halo = FFN_HALO
    seg = min(seq, tm)
    nseg = tm // seg
    inside = seq > tm

    def mixed(refs, rows):
        return jnp.concatenate([r[rows, :] for r in refs], axis=1)

    mix_in = mixed((a_ref, c_ref, h_ref), slice(None))
    x_in = x_ref[...]
    if inside:
        lo, hi = slice(BF16_SUBLANES - halo, BF16_SUBLANES), slice(0, halo)
        mix_in = jnp.concatenate([mixed((ap_ref, cp_ref, hp_ref), lo), mix_in,
                                  mixed((an_ref, cn_ref, hn_ref), hi)], axis=0)
        x_in = jnp.concatenate([xp_ref[...], x_in, xn_ref[...]], axis=0)
    x1 = x_in + mod_ref[2:3, :] * _dot(mix_in, wo_ref[...])

    def norm_mod(x):
        ms = jnp.mean(x * x, axis=-1, keepdims=True)
        h = x * lax.rsqrt(ms + NORM_EPS) * n2_ref[...]
        return h * (1.0 + mod_ref[4:5, :]) + mod_ref[3:4, :]

    zeros = jnp.zeros((halo, d), F32)
    if inside:
        tiles = seq // tm
        hx = norm_mod(x1)
        before = jnp.where(i % tiles == 0, 0.0, hx[:halo])
        after = jnp.where(i % tiles == tiles - 1, 0.0, hx[halo + tm:])
        hm = hx[halo:halo + tm]
        x1 = x1[halo:halo + tm]
    else:
        before, after = zeros, zeros
        hm = norm_mod(x1)
    parts = [before]
    for s in range(nseg):
        parts += [hm[s * seg:(s + 1) * seg], after if s == nseg - 1 else zeros]
    hb = jnp.concatenate(parts, axis=0).astype(BF16)

    p_ref[0] = _dot(hb, wup_ref[:, :ffn])
    p_ref[1] = _dot(hb, wup_ref[:, ffn:])
    ext = p_ref.shape[1]

    def dwconv(which, cols):
        wcols = slice(which * ffn + cols.start, which * ffn + cols.stop)
        w = dww_ref[:, wcols]
        p = p_ref[which, :, cols]
        prev = pltpu.roll(p, 1, axis=0)
        nxt = pltpu.roll(p, ext - 1, axis=0)
        out = []
        for s in range(nseg):
            o = halo + s * (seg + halo)
            out.append(prev[o:o + seg, :] * w[0:1] + p[o:o + seg, :] * w[1:2] + nxt[o:o + seg, :] * w[2:3]
                       + dwb_ref[:, wcols])
        return out[0] if nseg == 1 else jnp.concatenate(out, axis=0)

    for c0 in range(0, ffn, FFN_STRIP):
        cols = slice(c0, c0 + FFN_STRIP)
        gate = dwconv(1, cols)
        act_ref[:, cols] = (gate * _sigmoid(gate) * dwconv(0, cols)).astype(BF16)

    out = x1 + mod_ref[5:6, :] * _dot(act_ref[...], wdn_ref[...])
    if final:
        ms = jnp.mean(out * out, axis=-1, keepdims=True)
        out = out * lax.rsqrt(ms + NORM_EPS) * fn_ref[...]
    o_ref[...] = out


def _mix_ffn(x, attn, conv, hy, mod_l, cond_base, w_o, norm2, w_up, dw_w, dw_b, w_down, final_norm, seq, final):
    n, d = x.shape
    tm = TOKEN_TILE
    ffn = w_down.shape[0]
    assert ffn % FFN_STRIP == 0

    def tile_and_halos(arr, halo):
        w = arr.shape[1]
        per_tile = tm // halo
        last = n // halo - 1
        return [pl.BlockSpec((tm, w), lambda i: (i, 0)),
                pl.BlockSpec((halo, w), lambda i: (jnp.maximum(i * per_tile - 1, 0), 0)),
                pl.BlockSpec((halo, w), lambda i: (jnp.minimum((i + 1) * per_tile, last), 0))]

    return pl.pallas_call(
        functools.partial(_mix_ffn_kernel, seq=seq, final=final),
        grid=(n // tm,),
        in_specs=[
            *tile_and_halos(x, FFN_HALO),
            *tile_and_halos(attn, BF16_SUBLANES),
            *tile_and_halos(conv, BF16_SUBLANES),
            *tile_and_halos(hy, BF16_SUBLANES),
            _mod_spec(d, tm, seq, cond_base),
            pl.BlockSpec(w_o.shape, _const, pipeline_mode=_SINGLE),
            pl.BlockSpec((1, d), _const),
            pl.BlockSpec(w_up.shape, _const, pipeline_mode=_SINGLE),
            pl.BlockSpec((3, 2 * ffn), _const),
            pl.BlockSpec((1, 2 * ffn), _const),
            pl.BlockSpec(w_down.shape, _const, pipeline_mode=_SINGLE),
            pl.BlockSpec((1, d), _const),
        ],
        out_specs=pl.BlockSpec((tm, d), lambda i: (i, 0)),
        out_shape=jax.ShapeDtypeStruct((n, d), F32),
        scratch_shapes=[
            pltpu.VMEM((2, tm + (tm // min(seq, tm) + 1) * FFN_HALO, ffn), F32),
            pltpu.VMEM((tm, ffn), BF16),
        ],
        compiler_params=_params("parallel"),
        name="mix_ffn",
    )(x, x, x, attn, attn, attn, conv, conv, conv, hy, hy, hy, mod_l, w_o, norm2.reshape(1, d), w_up, dw_w,
      dw_b.reshape(1, -1), w_down, final_norm.reshape(1, d))


def kernel(x_prompt, x_sample, cache_k, cache_v, c, c_ctx, norm1, norm2, w_mod, b_mod, w_in, q_norm, k_norm,
           conv_dw_w, conv_dw_b, conv_gn_w, conv_gn_b, conv_pw_w, conv_pw_b, hy_short_w, hy_short_b,
           hy_f_w1, hy_f_b1, hy_f_w2, hy_f_b2, hy_f_w3, hy_f_b3, hy_freq, hy_bias, w_o, ffn_w_up,
           ffn_dw_w, ffn_dw_b, ffn_w_down, final_norm):
    batch, ctx_seq, d = x_prompt.shape
    dec_batch, lat_seq, _ = x_sample.shape
    past = cache_k.shape[2]
    depth = norm1.shape[0]
    conv_w = conv_pw_w.shape[1]
    hy_w = hy_bias.shape[2]
    assert 1 + dec_batch <= SUBLANES

    cond = jnp.concatenate([c_ctx[None, :], c, jnp.zeros((SUBLANES - 1 - dec_batch, d), F32)], axis=0)
    mod = _modulation(cond, w_mod, b_mod).transpose(0, 2, 1, 3)

    xs = [x_prompt.reshape(batch * ctx_seq, d), x_sample.reshape(dec_batch * lat_seq, d)]
    seqs = [ctx_seq, lat_seq]
    cond_bases = [0, 1]
    new_k, new_v = [], []
    for l in range(depth):
        w_in_l = w_in[l].astype(BF16)
        w_o_l = w_o[l].astype(BF16)
        w_up_l = ffn_w_up[l].astype(BF16)
        w_down_l = ffn_w_down[l].astype(BF16)
        filt = (hy_f_w1[l], hy_f_b1[l], hy_f_w2[l], hy_f_b2[l], hy_f_w3[l], hy_f_b3[l], hy_freq[l])
        for path in range(2):
            x, seq, base = xs[path], seqs[path], cond_bases[path]
            qt, k, kb, v, vt, zc, zh = _inproj(x, mod[l], base, norm1[l], w_in_l, q_norm[l], k_norm[l],
                                               seq, path == 1, 2 * conv_w, (HYENA_ORDER + 1) * hy_w)
            if path == 0:
                new_k.append(k.reshape(batch, ctx_seq, N_KV_HEADS, HEAD_DIM))
                new_v.append(v.reshape(batch, ctx_seq, N_KV_HEADS, HEAD_DIM))
                cache = None
            else:
                cache = (cache_k[:, l].reshape(dec_batch, past, KV_WIDTH).astype(BF16),
                         cache_v[:, l].reshape(dec_batch, past, KV_WIDTH).transpose(0, 2, 1).astype(BF16))
            attn = _attention(qt, kb, vt, seq, cache)
            conv = _conformer(zc, conv_dw_w[l], conv_dw_b[l], conv_gn_w[l], conv_gn_b[l], conv_pw_w[l],
                              conv_pw_b[l], seq)
            hy = _hyena(zh, hy_short_w[l], hy_short_b[l], hy_bias[l], filt, seq)
            xs[path] = _mix_ffn(x, attn, conv, hy, mod[l], base, w_o_l, norm2[l], w_up_l, ffn_dw_w[l],
                                ffn_dw_b[l], w_down_l, final_norm, seq, l == depth - 1)

    y_prompt = xs[0].reshape(batch, ctx_seq, d)
    y_sample = xs[1].reshape(dec_batch, lat_seq, d)
    return y_prompt, y_sample, jnp.stack(new_k, axis=1), jnp.stack(new_v, axis=1)
```

```python
import functools
import math

import numpy as np
import jax
import jax.numpy as jnp
from jax import lax
from jax.experimental import pallas as pl
from jax.experimental.pallas import tpu as pltpu

F32 = jnp.float32
BF16 = jnp.bfloat16

GRID_W = 64
N_HEADS = 8
N_KV_HEADS = 2
HEAD_DIM = 64
GROUP = N_HEADS // N_KV_HEADS
ATTN_WIDTH = N_HEADS * HEAD_DIM
KV_WIDTH = N_KV_HEADS * HEAD_DIM
CONV_GROUP_WIDTH = 64
CONV_KSIZE = 31
HYENA_ORDER = 2
HYENA_POS_BANDS = 16
HYENA_MIN_DECAY = math.log(1e-2) / 1.5
HYENA_MAX_DECAY = math.log(1e-2) / 0.3
ROPE_BASE = 10000.0
NORM_EPS = 1e-6

LANES = 128
SUBLANES = 8
BF16_SUBLANES = 16
VMEM_LIMIT = 56 * 1024 * 1024

TOKEN_TILE = 512
INPROJ_TILE = 1024
INPROJ_SLAB = 256
CONV_TILE = 512
CONV_HALO = 16
ATTN_Q_TILE = 1024
ATTN_K_CHUNK = 512
HY_MAX_BLOCK = 512
HY_SHORT_ROWS = 512
HY_STEP_ROWS = 4096
HY_SINGLE_BUFFER_BYTES = 8 * 1024 * 1024
FFN_HALO = 8
FFN_STRIP = 256

_SINGLE = pl.Buffered(1)


def _params(*sem):
    return pltpu.CompilerParams(dimension_semantics=sem, vmem_limit_bytes=VMEM_LIMIT)


def _split(x):
    hi = x.astype(BF16)
    lo = (x - hi.astype(F32)).astype(BF16)
    return hi, lo


def _dot(a, b):
    return jnp.dot(a, b, preferred_element_type=F32)


def _dot3(a, b_hi, b_lo):
    a_hi, a_lo = _split(a)
    return _dot(a_hi, b_hi) + _dot(a_hi, b_lo) + _dot(a_lo, b_hi)


def _dot3l(a_hi, a_lo, b):
    b_hi, b_lo = _split(b)
    return _dot(a_hi, b_hi) + _dot(a_lo, b_hi) + _dot(a_hi, b_lo)


def _sigmoid(x):
    return 1.0 / (1.0 + jnp.exp(-x))


def _const(*_):
    return (0, 0)


def _mod_kernel(cond_ref, w_ref, b_ref, o_ref):
    c = cond_ref[...]
    s = c * _sigmoid(c)
    o_ref[...] = _dot(s.astype(BF16), w_ref[...].astype(BF16)) + b_ref[...]


def _modulation(cond, w_mod, b_mod):
    depth, d, _ = w_mod.shape
    return pl.pallas_call(
        _mod_kernel,
        grid=(depth, 6),
        in_specs=[
            pl.BlockSpec((SUBLANES, d), lambda l, j: (0, 0)),
            pl.BlockSpec((None, d, d), lambda l, j: (l, 0, j)),
            pl.BlockSpec((None, 1, d), lambda l, j: (l, 0, j)),
        ],
        out_specs=pl.BlockSpec((None, None, SUBLANES, d), lambda l, j: (l, j, 0, 0)),
        out_shape=jax.ShapeDtypeStruct((depth, 6, SUBLANES, d), F32),
        compiler_params=_params("arbitrary", "arbitrary"),
        name="modulation",
    )(cond, w_mod, b_mod.reshape(depth, 1, 6 * d))


def _mod_spec(d, tile, seq, cond_base):
    per_seq = seq // tile if seq >= tile else 0
    if cond_base == 0:
        return pl.BlockSpec((None, 6, d), lambda i: (0, 0, 0))
    return pl.BlockSpec((None, 6, d), lambda i: (cond_base + i // per_seq, 0, 0))


def _inproj_kernel(x_ref, mod_ref, n1_ref, w_ref, qn_ref, kn_ref, g_ref, cos_ref, sin_ref,
                   qt_ref, k_ref, kb_ref, v_ref, vt_ref, zc_ref, zh_ref, *, rotary):
    x = x_ref[...]
    ms = jnp.mean(x * x, axis=-1, keepdims=True)
    h = x * lax.rsqrt(ms + NORM_EPS) * n1_ref[...]
    h = h * (1.0 + mod_ref[1:2, :]) + mod_ref[0:1, :]
    proj = _dot(h.astype(BF16), w_ref[...])

    def head_norm(t, w_ref_):
        width = t.shape[1]
        msq = _dot((t * t).astype(BF16), g_ref[:width, :width])
        t = t * lax.rsqrt(msq + NORM_EPS) * w_ref_[:, :width]
        if rotary:
            lane = lax.broadcasted_iota(jnp.int32, (1, width), 1)
            partner = jnp.where((lane % 32) < 16, pltpu.roll(t, width - 16, axis=1), pltpu.roll(t, 16, axis=1))
            t = t * cos_ref[:, :width] + partner * sin_ref[:, :width]
        return t

    scale = HEAD_DIM ** -0.5 * math.log2(math.e)
    slab = g_ref.shape[0]
    for c in range(ATTN_WIDTH // slab):
        sl = slice(c * slab, (c + 1) * slab)
        qn = head_norm(proj[:, sl], qn_ref) * scale
        qt_ref[sl, :] = qn.T.astype(qt_ref.dtype)

    kn = head_norm(proj[:, ATTN_WIDTH:ATTN_WIDTH + KV_WIDTH], kn_ref)
    k_ref[...] = kn
    kb_ref[...] = kn.astype(kb_ref.dtype)
    o = ATTN_WIDTH + KV_WIDTH
    v = proj[:, o:o + KV_WIDTH]
    v_ref[...] = v
    vt_ref[...] = v.T.astype(vt_ref.dtype)
    o += KV_WIDTH
    zc_ref[...] = proj[:, o:o + zc_ref.shape[1]].astype(zc_ref.dtype)
    o += zc_ref.shape[1]
    zh_ref[...] = proj[:, o:o + zh_ref.shape[1]].astype(zh_ref.dtype)


def _rope_tables(length, width):
    pos = np.arange(length)
    rows, cols = pos // GRID_W, pos % GRID_W
    half = HEAD_DIM // 2
    inv = ROPE_BASE ** (-np.arange(0, half, 2, dtype=np.float64) / half)
    d = np.arange(HEAD_DIM)
    p = np.where(d[None, :] < half, rows[:, None], cols[:, None]).astype(np.float64)
    ang = p * inv[d % (half // 2)][None, :]
    sign = np.where((d % half) < half // 2, -1.0, 1.0)[None, :]
    cos = np.tile(np.cos(ang), (1, width // HEAD_DIM))
    sin = np.tile(np.sin(ang) * sign, (1, width // HEAD_DIM))
    return jnp.asarray(cos, F32), jnp.asarray(sin, F32)


def _group_mean_matrix(width, group):
    idx = np.arange(width) // group
    return jnp.asarray((idx[:, None] == idx[None, :]).astype(np.float32) / group, BF16)


def _inproj(x, mod_l, cond_base, norm1, w_in, q_norm, k_norm, seq, rotary, zc_w, zh_w):
    n, d = x.shape
    tm = INPROJ_TILE
    per_seq = max(seq // tm, 1)
    slab = INPROJ_SLAB
    cos, sin = _rope_tables(seq if rotary else tm, slab)
    g = _group_mean_matrix(slab, HEAD_DIM)
    tab = (lambda i: (i % per_seq, 0)) if rotary else _const
    row = lambda i: (i, 0)
    col = lambda i: (0, i)
    return pl.pallas_call(
        functools.partial(_inproj_kernel, rotary=rotary),
        grid=(n // tm,),
        in_specs=[
            pl.BlockSpec((tm, d), row),
            _mod_spec(d, tm, seq, cond_base),
            pl.BlockSpec((1, d), _const),
            pl.BlockSpec(w_in.shape, _const),
            pl.BlockSpec((1, slab), _const),
            pl.BlockSpec((1, slab), _const),
            pl.BlockSpec((slab, slab), _const),
            pl.BlockSpec((tm, slab), tab),
            pl.BlockSpec((tm, slab), tab),
        ],
        out_specs=[
            pl.BlockSpec((ATTN_WIDTH, tm), col),
            pl.BlockSpec((tm, KV_WIDTH), row),
            pl.BlockSpec((tm, KV_WIDTH), row),
            pl.BlockSpec((tm, KV_WIDTH), row),
            pl.BlockSpec((KV_WIDTH, tm), col),
            pl.BlockSpec((tm, zc_w), row),
            pl.BlockSpec((tm, zh_w), row),
        ],
        out_shape=[
            jax.ShapeDtypeStruct((ATTN_WIDTH, n), BF16),
            jax.ShapeDtypeStruct((n, KV_WIDTH), F32),
            jax.ShapeDtypeStruct((n, KV_WIDTH), BF16),
            jax.ShapeDtypeStruct((n, KV_WIDTH), F32),
            jax.ShapeDtypeStruct((KV_WIDTH, n), BF16),
            jax.ShapeDtypeStruct((n, zc_w), BF16),
            jax.ShapeDtypeStruct((n, zh_w), BF16),
        ],
        compiler_params=_params("parallel"),
        name="inproj",
    )(x, mod_l, norm1.reshape(1, d), w_in,
      jnp.tile(q_norm, slab // HEAD_DIM).reshape(1, slab),
      jnp.tile(k_norm, slab // HEAD_DIM).reshape(1, slab), g, cos, sin)


def _attn_kernel(*refs, chunks, cache_chunks):
    if cache_chunks:
        qt_ref, k_ref, vt_ref, ck_ref, cvt_ref, o_ref = refs
    else:
        qt_ref, k_ref, vt_ref, o_ref = refs
    tq = qt_ref.shape[1]
    rows = GROUP * HEAD_DIM
    heads = qt_ref.shape[0] // rows

    def ones_rows(size):
        r = lax.broadcasted_iota(jnp.int32, (BF16_SUBLANES, size), 0)
        return jnp.where(r == 0, 1.0, 0.0).astype(BF16)

    def update(s, v, m, acc):
        m_new = jnp.maximum(m, jnp.max(s, axis=0, keepdims=True))
        alpha = jnp.exp2(m - m_new)
        p = jnp.exp2(s - m_new).astype(BF16)
        v_aug = jnp.concatenate([v, ones_rows(v.shape[1])], axis=0)
        return m_new, acc * alpha + _dot(v_aug, p)

    for hh in range(heads):
        h = hh if heads == N_KV_HEADS else pl.program_id(1)
        q4 = jnp.concatenate([qt_ref[hh * rows + g * HEAD_DIM:hh * rows + (g + 1) * HEAD_DIM, :]
                              for g in range(GROUP)], axis=1)
        zero = jnp.zeros_like(q4)
        q_pad = jnp.where(h == 0, jnp.concatenate([q4, zero], axis=0), jnp.concatenate([zero, q4], axis=0))
        vrows = slice(hh * HEAD_DIM, (hh + 1) * HEAD_DIM)

        m = jnp.full((1, GROUP * tq), -1e30, F32)
        acc = jnp.zeros((HEAD_DIM + BF16_SUBLANES, GROUP * tq), F32)
        for start, size in chunks:
            m, acc = update(_dot(k_ref[start:start + size, :], q_pad), vt_ref[vrows, start:start + size], m, acc)
        for start, size in cache_chunks:
            m, acc = update(_dot(ck_ref[start:start + size, :], q_pad), cvt_ref[vrows, start:start + size], m, acc)
        out = acc[:HEAD_DIM] / acc[HEAD_DIM:HEAD_DIM + 1]
        out = jnp.concatenate([out[:, g * tq:(g + 1) * tq] for g in range(GROUP)], axis=0)
        o_ref[:, hh * rows:(hh + 1) * rows] = out.T.astype(o_ref.dtype)


def _chunks(n):
    return [(s, min(ATTN_K_CHUNK, n - s)) for s in range(0, n, ATTN_K_CHUNK)]


def _attention(qt, kb, vt, seq, cache=None):
    n = kb.shape[0]
    nb = n // seq
    tq = min(ATTN_Q_TILE, seq)
    nq = seq // tq
    hps = N_KV_HEADS if seq <= ATTN_K_CHUNK else 1
    rows = hps * GROUP * HEAD_DIM
    in_specs = [
        pl.BlockSpec((rows, tq), lambda b, h, i: (h, b * nq + i)),
        pl.BlockSpec((seq, KV_WIDTH), lambda b, h, i: (b, 0)),
        pl.BlockSpec((hps * HEAD_DIM, seq), lambda b, h, i: (h, b)),
    ]
    args = [qt, kb, vt]
    cache_chunks = []
    if cache is not None:
        past = cache[0].shape[1]
        in_specs += [
            pl.BlockSpec((None, past, KV_WIDTH), lambda b, h, i: (b, 0, 0)),
            pl.BlockSpec((None, hps * HEAD_DIM, past), lambda b, h, i: (b, h, 0)),
        ]
        args += list(cache)
        cache_chunks = _chunks(past)
    return pl.pallas_call(
        functools.partial(_attn_kernel, chunks=_chunks(seq), cache_chunks=cache_chunks),
        grid=(nb, N_KV_HEADS // hps, nq),
        in_specs=in_specs,
        out_specs=pl.BlockSpec((tq, rows), lambda b, h, i: (b * nq + i, h)),
        out_shape=jax.ShapeDtypeStruct((n, ATTN_WIDTH), BF16),
        compiler_params=_params("parallel", "parallel", "parallel"),
        name="attention",
    )(*args)


def _conformer_kernel(z_ref, zp_ref, zn_ref, dww_ref, dwb_ref, gnw_ref, gnb_ref, g_ref, pww_ref, pwb_ref,
                      o_ref, ubuf_ref, cbuf_ref, *, tiles_per_seq):
    i = pl.program_id(0)
    tt, c = o_ref.shape
    halo = CONV_HALO
    first = (i % tiles_per_seq) == 0
    last = (i % tiles_per_seq) == tiles_per_seq - 1

    def glu(ref):
        z = ref[...].astype(F32)
        return z[:, :c] * _sigmoid(z[:, c:])

    ubuf_ref[0, 0:halo, :] = jnp.where(first, 0.0, glu(zp_ref))
    ubuf_ref[0, halo:halo + tt, :] = glu(z_ref)
    ubuf_ref[0, halo + tt:, :] = jnp.where(last, 0.0, glu(zn_ref))
    span = tt + 2 * halo - SUBLANES
    for s in range(1, SUBLANES):
        ubuf_ref[s, 0:span, :] = ubuf_ref[0, s:s + span, :]

    pad = (CONV_KSIZE - 1) // 2
    rows = 64
    for r in range(tt // rows):
        acc = jnp.zeros((rows, c), F32) + dwb_ref[...]
        for k in range(CONV_KSIZE):
            off = halo + r * rows - pad + k
            s = off % SUBLANES
            acc = acc + ubuf_ref[s, off - s:off - s + rows, :] * dww_ref[k:k + 1, :]
        cbuf_ref[r * rows:(r + 1) * rows, :] = acc

    g = g_ref[...]
    u = cbuf_ref[...]
    hi, lo = _split(u)
    mu = _dot(hi, g) + _dot(lo, g)
    dlt = u - mu
    var = _dot((dlt * dlt).astype(BF16), g)
    un = dlt * lax.rsqrt(var + NORM_EPS) * gnw_ref[...] + gnb_ref[...]
    act = un * _sigmoid(un)
    o_ref[...] = (_dot(act.astype(BF16), pww_ref[...]) + pwb_ref[...]).astype(o_ref.dtype)


def _conformer(zc, dw_w, dw_b, gn_w, gn_b, pw_w, pw_b, seq):
    n, c2 = zc.shape
    c = c2 // 2
    tt = min(CONV_TILE, seq)
    hb = tt // CONV_HALO
    n_halo_blocks = n // CONV_HALO
    dw_w = jnp.concatenate([dw_w, jnp.zeros((1, c), F32)], axis=0)
    g = _group_mean_matrix(c, CONV_GROUP_WIDTH)
    return pl.pallas_call(
        functools.partial(_conformer_kernel, tiles_per_seq=seq // tt),
        grid=(n // tt,),
        in_specs=[
            pl.BlockSpec((tt, c2), lambda i: (i, 0)),
            pl.BlockSpec((CONV_HALO, c2), lambda i: (jnp.maximum(i * hb - 1, 0), 0)),
            pl.BlockSpec((CONV_HALO, c2), lambda i: (jnp.minimum((i + 1) * hb, n_halo_blocks - 1), 0)),
            pl.BlockSpec((CONV_KSIZE + 1, c), _const),
            pl.BlockSpec((1, c), _const),
            pl.BlockSpec((1, c), _const),
            pl.BlockSpec((1, c), _const),
            pl.BlockSpec((c, c), _const),
            pl.BlockSpec((c, c), _const),
            pl.BlockSpec((1, c), _const),
        ],
        out_specs=pl.BlockSpec((tt, c), lambda i: (i, 0)),
        out_shape=jax.ShapeDtypeStruct((n, c), BF16),
        scratch_shapes=[pltpu.VMEM((SUBLANES, tt + 2 * CONV_HALO, c), F32), pltpu.VMEM((tt, c), F32)],
        compiler_params=_params("parallel"),
        name="conformer",
    )(zc, zc, zc, dw_w, dw_b.reshape(1, c), gn_w.reshape(1, c), gn_b.reshape(1, c), g,
      pw_w.astype(BF16), pw_b.reshape(1, c))


def _fpad(t):
    return -(-(t + 1) // SUBLANES) * SUBLANES


def _dft_tables(t_blk):
    fpad = _fpad(t_blk)
    t = np.arange(t_blk)
    k = np.arange(fpad)
    ang = ((k[:, None] * t[None, :]) % (2 * t_blk)) * (2.0 * np.pi / (2 * t_blk))
    live = (k <= t_blk)[:, None]
    fwd = np.concatenate([np.where(live, np.cos(ang), 0.0), np.where(live, -np.sin(ang), 0.0)], axis=0)
    ck = np.where((k == 0) | (k == t_blk), 1.0, 2.0)[:, None] / (2 * t_blk)
    inv = np.concatenate([np.where(live, ck * np.cos(ang), 0.0), np.where(live, -ck * np.sin(ang), 0.0)], axis=0).T

    return jnp.asarray(fwd, BF16), jnp.asarray(inv, BF16)


def _filter_features(length):
    lag = np.arange(2 * length) - length
    idx = np.minimum(np.abs(lag), length - 1)
    t = np.linspace(0.0, 1.0, length)[idx]
    w_ang = 2.0 * np.pi * idx / length
    bands = np.linspace(1e-4, HYENA_POS_BANDS - 1, HYENA_POS_BANDS)
    z = np.zeros((2 * length, LANES), np.float64)
    z[:, 0] = t
    z[:, 1:1 + HYENA_POS_BANDS] = np.cos(bands[None, :] * w_ang[:, None])
    z[:, 1 + HYENA_POS_BANDS:1 + 2 * HYENA_POS_BANDS] = np.sin(-bands[None, :] * w_ang[:, None])
    z[:, 1 + 2 * HYENA_POS_BANDS] = (lag > -length).astype(np.float64)
    return jnp.asarray(z, F32), jnp.asarray(z.T, F32)


def _hyena_filter_kernel(z_ref, zt_ref, w1t_ref, b1_ref, w2t_ref, b2_ref, fr_ref, w3_ref, b3_ref, dl_ref,
                         fh_ref, h_ref, prev_ref):
    s = pl.program_id(0)
    fpad = prev_ref.shape[0] // 2
    z = z_ref[...]
    fr = fr_ref[...]

    def dense_t(wt_ref, a, b_ref_):
        w_hi, w_lo = _split(wt_ref[...])
        return _dot3l(w_hi, w_lo, a) + b_ref_[...]

    hid = jnp.sin(fr * dense_t(w1t_ref, zt_ref[...], b1_ref))
    hid = jnp.sin(fr * dense_t(w2t_ref, hid, b2_ref)).T
    w_hi, w_lo = _split(w3_ref[...])
    h = _dot3(hid, w_hi, w_lo) + b3_ref[...]
    t = z[:, 0:1]
    live = z[:, 1 + 2 * HYENA_POS_BANDS:2 + 2 * HYENA_POS_BANDS]
    h = h * jnp.exp(-t * dl_ref[...]) * live
    spec = _dot(fh_ref[...], h.astype(BF16))

    @pl.when(s == 0)
    def _():
        prev_ref[...] = jnp.zeros_like(prev_ref)

    row = lax.broadcasted_iota(jnp.int32, (2 * fpad, 1), 0)
    sign = (1 - 2 * ((row % fpad) % 2)).astype(F32)
    h_ref[...] = spec + sign * prev_ref[...]
    prev_ref[...] = spec


def _hyena_filters(length, t_blk, w1, b1, w2, b2, w3, b3, freq, dft_fwd):
    nb = length // t_blk
    fpad = _fpad(t_blk)
    fh = w2.shape[0]
    oc = w3.shape[1] // 2
    c = oc // HYENA_ORDER
    z, zt = _filter_features(length)
    w1t = jnp.concatenate([w1, jnp.zeros((LANES - w1.shape[0], fh), F32)], axis=0).T
    w3d = w3.reshape(fh, HYENA_ORDER, 2, c).transpose(2, 0, 1, 3).reshape(2, fh, oc)
    b3d = b3.reshape(HYENA_ORDER, 2, c).transpose(1, 0, 2).reshape(2, 1, oc)
    deltas = np.abs(np.linspace(HYENA_MIN_DECAY, HYENA_MAX_DECAY, c))
    dl = jnp.asarray(np.tile(deltas, HYENA_ORDER)[None, :], F32)
    direction = lambda s: (jnp.where(s >= nb, 0, 1), 0, 0)
    return pl.pallas_call(
        _hyena_filter_kernel,
        grid=(2 * nb,),
        in_specs=[
            pl.BlockSpec((t_blk, LANES), lambda s: (s, 0)),
            pl.BlockSpec((LANES, t_blk), lambda s: (0, s)),
            pl.BlockSpec((fh, LANES), _const),
            pl.BlockSpec((fh, 1), _const),
            pl.BlockSpec((fh, fh), _const),
            pl.BlockSpec((fh, 1), _const),
            pl.BlockSpec((fh, 1), _const),
            pl.BlockSpec((None, fh, oc), direction),
            pl.BlockSpec((None, 1, oc), direction),
            pl.BlockSpec((1, oc), _const),
            pl.BlockSpec((2 * fpad, t_blk), _const),
        ],
        out_specs=pl.BlockSpec((None, 2 * fpad, oc), lambda s: (s, 0, 0)),
        out_shape=jax.ShapeDtypeStruct((2 * nb, 2 * fpad, oc), F32),
        scratch_shapes=[pltpu.VMEM((2 * fpad, oc), F32)],
        compiler_params=_params("arbitrary"),
        name="hyena_filter",
    )(z, zt, w1t, b1.reshape(fh, 1), w2.T, b2.reshape(fh, 1), freq.reshape(fh, 1), w3d, b3d, dl,
      dft_fwd)


def _hyena_conv_kernel(x_ref, gz_ref, sw_ref, sb_ref, gw_ref, gb_ref, bias_ref, h_ref, f_ref, g_ref,
                       o_ref, xs_ref, u_ref, y_ref, *, length, nb, short_signal, rc):
    rows, c = x_ref.shape
    t_blk = length // nb
    n_blocks = rows // t_blk
    fpad = y_ref.shape[0] // 2
    sr = min(HY_SHORT_ROWS, rows)

    def short_conv(ref, w_ref_, b_ref_, r0):
        a = ref[pl.ds(r0, sr), :].astype(F32)
        w = BF16_SUBLANES
        before = ref[pl.ds(pl.multiple_of(jnp.maximum(r0 - w, 0), w), w), :].astype(F32)[w - 1:, :]
        after = ref[pl.ds(pl.multiple_of(jnp.minimum(r0 + sr, rows - w), w), w), :].astype(F32)[:1, :]
        prev = pltpu.roll(a, 1, axis=0)
        nxt = pltpu.roll(a, sr - 1, axis=0)
        if length >= sr:
            before = jnp.where(r0 % length == 0, 0.0, before)
            after = jnp.where((r0 + sr) % length == 0, 0.0, after)
            starts = {0: before}
            ends = {sr: after}
        else:
            starts = {s: 0.0 for s in range(0, sr, length)}
            ends = {s + length: 0.0 for s in range(0, sr, length)}
        sub = lax.broadcasted_iota(jnp.int32, (SUBLANES, 1), 0)

        def patched(v, at, pick):
            pieces, done = [], 0
            for r, val in sorted(at.items()):
                g0 = (r if pick == 0 else r - SUBLANES)
                if g0 > done:
                    pieces.append(v[done:g0])
                pieces.append(jnp.where(sub == pick, val, v[g0:g0 + SUBLANES]))
                done = g0 + SUBLANES
            if done < v.shape[0]:
                pieces.append(v[done:])
            return jnp.concatenate(pieces, axis=0)

        prev = patched(prev, starts, 0)
        nxt = patched(nxt, ends, SUBLANES - 1)
        return prev * w_ref_[0:1, :] + a * w_ref_[1:2, :] + nxt * w_ref_[2:3, :] + b_ref_[...]

    def load_signal(r, _):
        r0 = pl.multiple_of(r * sr, sr)
        if short_signal:
            xs_ref[pl.ds(r0, sr), :] = short_conv(x_ref, sw_ref, sb_ref, r0)
        else:
            xs_ref[pl.ds(r0, sr), :] = x_ref[pl.ds(r0, sr), :].astype(F32)
        return 0

    lax.fori_loop(0, rows // sr, load_signal, 0)

    def fwd(j, _):
        xj = xs_ref[pl.ds(pl.multiple_of(j * t_blk, t_blk), t_blk), :]
        u_ref[j] = _dot(f_ref[...], xj.astype(BF16))
        return 0

    lax.fori_loop(0, n_blocks, fwd, 0)

    def out_block(sb, _):
        i = sb % nb
        first = sb - i

        def mac(r, _):
            re = pl.ds(pl.multiple_of(r * rc, SUBLANES), rc)
            im = pl.ds(pl.multiple_of(fpad + r * rc, SUBLANES), rc)
            are = jnp.zeros((rc, c), F32)
            aim = jnp.zeros((rc, c), F32)
            for j in range(nb):
                d = i - j + nb
                hre, him = h_ref[d, re, :], h_ref[d, im, :]
                ure, uim = u_ref[first + j, re, :], u_ref[first + j, im, :]
                are = are + hre * ure - him * uim
                aim = aim + hre * uim + him * ure
            y_ref[re, :] = are
            y_ref[im, :] = aim
            return 0

        lax.fori_loop(0, fpad // rc, mac, 0)
        blk = pl.ds(pl.multiple_of(sb * t_blk, t_blk), t_blk)
        conv = _dot(g_ref[...], y_ref[...].astype(BF16))
        xs_ref[blk, :] = conv + xs_ref[blk, :] * bias_ref[...]
        return 0

    lax.fori_loop(0, n_blocks, out_block, 0)

    def gate(r, _):
        r0 = pl.multiple_of(r * sr, sr)
        o_ref[pl.ds(r0, sr), :] = (xs_ref[pl.ds(r0, sr), :] * short_conv(gz_ref, gw_ref, gb_ref, r0)).astype(o_ref.dtype)
        return 0

    lax.fori_loop(0, rows // sr, gate, 0)


def _hyena_conv(x, x_col, zh, gate_col, short_w, short_b, bias, spectra, order, length, t_blk,
                dft_fwd, dft_inv, short_signal):
    c = bias.shape[0]
    n = zh.shape[0]
    nb = length // t_blk
    fpad = _fpad(t_blk)
    rc = max(r for r in (8, 24, 40) if fpad % r == 0)
    rows = min(max(length, HY_STEP_ROWS), n)
    big = _SINGLE if 2 * nb * 2 * fpad * c * 4 > HY_SINGLE_BUFFER_BYTES else None
    return pl.pallas_call(
        functools.partial(_hyena_conv_kernel, length=length, nb=nb, short_signal=short_signal, rc=rc),
        grid=(n // rows,),
        in_specs=[
            pl.BlockSpec((rows, c), lambda b: (b, x_col)),
            pl.BlockSpec((rows, c), lambda b: (b, gate_col)),
            pl.BlockSpec((3, c), lambda b: (0, x_col)),
            pl.BlockSpec((1, c), lambda b: (0, x_col)),
            pl.BlockSpec((3, c), lambda b: (0, gate_col)),
            pl.BlockSpec((1, c), lambda b: (0, gate_col)),
            pl.BlockSpec((1, c), lambda b: (0, 0)),
            pl.BlockSpec((2 * nb, 2 * fpad, c), lambda b: (0, 0, order), pipeline_mode=big),
            pl.BlockSpec((2 * fpad, t_blk), _const),
            pl.BlockSpec((t_blk, 2 * fpad), _const),
        ],
        out_specs=pl.BlockSpec((rows, c), lambda b: (b, 0)),
        out_shape=jax.ShapeDtypeStruct((n, c), BF16),
        scratch_shapes=[
            pltpu.VMEM((rows, c), F32),
            pltpu.VMEM((rows // t_blk, 2 * fpad, c), F32),
            pltpu.VMEM((2 * fpad, c), F32),
        ],
        compiler_params=_params("arbitrary"),
        name="hyena_conv",
    )(x, zh, short_w, short_b.reshape(1, -1), short_w, short_b.reshape(1, -1), bias.reshape(1, c), spectra,
      dft_fwd, dft_inv)


def _hyena(zh, short_w, short_b, hy_bias, filt, length):
    t_blk = min(HY_MAX_BLOCK, length)
    dft_fwd, dft_inv = _dft_tables(t_blk)
    spectra = _hyena_filters(length, t_blk, *filt, dft_fwd)
    u1 = _hyena_conv(zh, 0, zh, 1, short_w, short_b, hy_bias[0], spectra, 0, length, t_blk,
                     dft_fwd, dft_inv, True)
    return _hyena_conv(u1, 0, zh, 2, short_w, short_b, hy_bias[1], spectra, 1, length, t_blk,
                       dft_fwd, dft_inv, False)


def _mix_ffn_kernel(x_ref, xp_ref, xn_ref, a_ref, ap_ref, an_ref, c_ref, cp_ref, cn_ref, h_ref, hp_ref, hn_ref,
                    mod_ref, wo_ref, n2_ref, wup_ref, dww_ref, dwb_ref, wdn_ref, fn_ref,
                    o_ref, p_ref, act_ref, *, seq, final):
    i = pl.program_id(0)
    tm, d = x_ref.shape
    ffn = wdn_ref.shape[0]
    halo = FFN_HALO
    seg = min(seq, tm)
    nseg = tm // seg
    inside = seq > tm

    def mixed(refs, rows):
        return jnp.concatenate([r[rows, :] for r in refs], axis=1)

    mix_in = mixed((a_ref, c_ref, h_ref), slice(None))
    x_in = x_ref[...]
    if inside:
        lo, hi = slice(BF16_SUBLANES - halo, BF16_SUBLANES), slice(0, halo)
        mix_in = jnp.concatenate([mixed((ap_ref, cp_ref, hp_ref), lo), mix_in,
                                  mixed((an_ref, cn_ref, hn_ref), hi)], axis=0)
        x_in = jnp.concatenate([xp_ref[...], x_in, xn_ref[...]], axis=0)
    x1 = x_in + mod_ref[2:3, :] * _dot(mix_in, wo_ref[...])

    def norm_mod(x):
        ms = jnp.mean(x * x, axis=-1, keepdims=True)
        h = x * lax.rsqrt(ms + NORM_EPS) * n2_ref[...]
        return h * (1.0 + mod_ref[4:5, :]) + mod_ref[3:4, :]

    zeros = jnp.zeros((halo, d), F32)
    if inside:
        tiles = seq // tm
        hx = norm_mod(x1)
        before = jnp.where(i % tiles == 0, 0.0, hx[:halo])
        after = jnp.where(i % tiles == tiles - 1, 0.0, hx[halo + tm:])
        hm = hx[halo:halo + tm]
        x1 = x1[halo:halo + tm]
    else:
        before, after = zeros, zeros
        hm = norm_mod(x1)
    parts = [before]
    for s in range(nseg):
        parts += [hm[s * seg:(s + 1) * seg], after if s == nseg - 1 else zeros]
    hb = jnp.concatenate(parts, axis=0).astype(BF16)

    p_ref[0] = _dot(hb, wup_ref[:, :ffn])
    p_ref[1] = _dot(hb, wup_ref[:, ffn:])
    ext = p_ref.shape[1]

    def dwconv(which, cols):
        wcols = slice(which * ffn + cols.start, which * ffn + cols.stop)
        w = dww_ref[:, wcols]
        p = p_ref[which, :, cols]
        prev = pltpu.roll(p, 1, axis=0)
        nxt = pltpu.roll(p, ext - 1, axis=0)
        out = []
        for s in range(nseg):
            o = halo + s * (seg + halo)
            out.append(prev[o:o + seg, :] * w[0:1] + p[o:o + seg, :] * w[1:2] + nxt[o:o + seg, :] * w[2:3]
                       + dwb_ref[:, wcols])
        return out[0] if nseg == 1 else jnp.concatenate(out, axis=0)

    for c0 in range(0, ffn, FFN_STRIP):
        cols = slice(c0, c0 + FFN_STRIP)
        gate = dwconv(1, cols)
        act_ref[:, cols] = (gate * _sigmoid(gate) * dwconv(0, cols)).astype(BF16)

    out = x1 + mod_ref[5:6, :] * _dot(act_ref[...], wdn_ref[...])
    if final:
        ms = jnp.mean(out * out, axis=-1, keepdims=True)
        out = out * lax.rsqrt(ms + NORM_EPS) * fn_ref[...]
    o_ref[...] = out


def _mix_ffn(x, attn, conv, hy, mod_l, cond_base, w_o, norm2, w_up, dw_w, dw_b, w_down, final_norm, seq, final):
    n, d = x.shape
    tm = TOKEN_TILE
    ffn = w_down.shape[0]
    assert ffn % FFN_STRIP == 0

    def tile_and_halos(arr, halo):
        w = arr.shape[1]
        per_tile = tm // halo
        last = n // halo - 1
        return [pl.BlockSpec((tm, w), lambda i: (i, 0)),
                pl.BlockSpec((halo, w), lambda i: (jnp.maximum(i * per_tile - 1, 0), 0)),
                pl.BlockSpec((halo, w), lambda i: (jnp.minimum((i + 1) * per_tile, last), 0))]

    return pl.pallas_call(
        functools.partial(_mix_ffn_kernel, seq=seq, final=final),
        grid=(n // tm,),
        in_specs=[
            *tile_and_halos(x, FFN_HALO),
            *tile_and_halos(attn, BF16_SUBLANES),
            *tile_and_halos(conv, BF16_SUBLANES),
            *tile_and_halos(hy, BF16_SUBLANES),
            _mod_spec(d, tm, seq, cond_base),
            pl.BlockSpec(w_o.shape, _const, pipeline_mode=_SINGLE),
            pl.BlockSpec((1, d), _const),
            pl.BlockSpec(w_up.shape, _const, pipeline_mode=_SINGLE),
            pl.BlockSpec((3, 2 * ffn), _const),
            pl.BlockSpec((1, 2 * ffn), _const),
            pl.BlockSpec(w_down.shape, _const, pipeline_mode=_SINGLE),
            pl.BlockSpec((1, d), _const),
        ],
        out_specs=pl.BlockSpec((tm, d), lambda i: (i, 0)),
        out_shape=jax.ShapeDtypeStruct((n, d), F32),
        scratch_shapes=[
            pltpu.VMEM((2, tm + (tm // min(seq, tm) + 1) * FFN_HALO, ffn), F32),
            pltpu.VMEM((tm, ffn), BF16),
        ],
        compiler_params=_params("parallel"),
        name="mix_ffn",
    )(x, x, x, attn, attn, attn, conv, conv, conv, hy, hy, hy, mod_l, w_o, norm2.reshape(1, d), w_up, dw_w,
      dw_b.reshape(1, -1), w_down, final_norm.reshape(1, d))


def kernel(x_prompt, x_sample, cache_k, cache_v, c, c_ctx, norm1, norm2, w_mod, b_mod, w_in, q_norm, k_norm,
           conv_dw_w, conv_dw_b, conv_gn_w, conv_gn_b, conv_pw_w, conv_pw_b, hy_short_w, hy_short_b,
           hy_f_w1, hy_f_b1, hy_f_w2, hy_f_b2, hy_f_w3, hy_f_b3, hy_freq, hy_bias, w_o, ffn_w_up,
           ffn_dw_w, ffn_dw_b, ffn_w_down, final_norm):
    batch, ctx_seq, d = x_prompt.shape
    dec_batch, lat_seq, _ = x_sample.shape
    past = cache_k.shape[2]
    depth = norm1.shape[0]
    conv_w = conv_pw_w.shape[1]
    hy_w = hy_bias.shape[2]
    assert 1 + dec_batch <= SUBLANES

    cond = jnp.concatenate([c_ctx[None, :], c, jnp.zeros((SUBLANES - 1 - dec_batch, d), F32)], axis=0)
    mod = _modulation(cond, w_mod, b_mod).transpose(0, 2, 1, 3)

    xs = [x_prompt.reshape(batch * ctx_seq, d), x_sample.reshape(dec_batch * lat_seq, d)]
    seqs = [ctx_seq, lat_seq]
    cond_bases = [0, 1]
    new_k, new_v = [], []
    for l in range(depth):
        w_in_l = w_in[l].astype(BF16)
        w_o_l = w_o[l].astype(BF16)
        w_up_l = ffn_w_up[l].astype(BF16)
        w_down_l = ffn_w_down[l].astype(BF16)
        filt = (hy_f_w1[l], hy_f_b1[l], hy_f_w2[l], hy_f_b2[l], hy_f_w3[l], hy_f_b3[l], hy_freq[l])
        for path in range(2):
            x, seq, base = xs[path], seqs[path], cond_bases[path]
            qt, k, kb, v, vt, zc, zh = _inproj(x, mod[l], base, norm1[l], w_in_l, q_norm[l], k_norm[l],
                                               seq, path == 1, 2 * conv_w, (HYENA_ORDER + 1) * hy_w)
            if path == 0:
                new_k.append(k.reshape(batch, ctx_seq, N_KV_HEADS, HEAD_DIM))
                new_v.append(v.reshape(batch, ctx_seq, N_KV_HEADS, HEAD_DIM))
                cache = None
            else:
                cache = (cache_k[:, l].reshape(dec_batch, past, KV_WIDTH).astype(BF16),
                         cache_v[:, l].reshape(dec_batch, past, KV_WIDTH).transpose(0, 2, 1).astype(BF16))
            attn = _attention(qt, kb, vt, seq, cache)
            conv = _conformer(zc, conv_dw_w[l], conv_dw_b[l], conv_gn_w[l], conv_gn_b[l], conv_pw_w[l],
                              conv_pw_b[l], seq)
            hy = _hyena(zh, hy_short_w[l], hy_short_b[l], hy_bias[l], filt, seq)
            xs[path] = _mix_ffn(x, attn, conv, hy, mod[l], base, w_o_l, norm2[l], w_up_l, ffn_dw_w[l],
                                ffn_dw_b[l], w_down_l, final_norm, seq, l == depth - 1)

    y_prompt = xs[0].reshape(batch, ctx_seq, d)
    y_sample = xs[1].reshape(dec_batch, lat_seq, d)
    return y_prompt, y_sample, jnp.stack(new_k, axis=1), jnp.stack(new_v, axis=1)
```

```python
import functools
import math

import numpy as np
import jax
import jax.numpy as jnp
from jax import lax
from jax.experimental import pallas as pl
from jax.experimental.pallas import tpu as pltpu

F32 = jnp.float32
BF16 = jnp.bfloat16

GRID_W = 64
N_HEADS = 8
N_KV_HEADS = 2
HEAD_DIM = 64
GROUP = N_HEADS // N_KV_HEADS
ATTN_WIDTH = N_HEADS * HEAD_DIM
KV_WIDTH = N_KV_HEADS * HEAD_DIM
CONV_GROUP_WIDTH = 64
CONV_KSIZE = 31
HYENA_ORDER = 2
HYENA_POS_BANDS = 16
HYENA_MIN_DECAY = math.log(1e-2) / 1.5
HYENA_MAX_DECAY = math.log(1e-2) / 0.3
ROPE_BASE = 10000.0
NORM_EPS = 1e-6

LANES = 128
SUBLANES = 8
BF16_SUBLANES = 16
VMEM_LIMIT = 56 * 1024 * 1024

TOKEN_TILE = 512
INPROJ_TILE = 1024
INPROJ_SLAB = 256
CONV_TILE = 512
CONV_HALO = 16
ATTN_Q_TILE = 1024
ATTN_K_CHUNK = 512
HY_MAX_BLOCK = 512
HY_SHORT_ROWS = 512
HY_STEP_ROWS = 4096
HY_SINGLE_BUFFER_BYTES = 8 * 1024 * 1024
FFN_HALO = 8
FFN_STRIP = 256

_SINGLE = pl.Buffered(1)


def _params(*sem):
    return pltpu.CompilerParams(dimension_semantics=sem, vmem_limit_bytes=VMEM_LIMIT)


def _split(x):
    hi = x.astype(BF16)
    lo = (x - hi.astype(F32)).astype(BF16)
    return hi, lo


def _dot(a, b):
    return jnp.dot(a, b, preferred_element_type=F32)


def _dot3l(a_hi, a_lo, b):
    b_hi, b_lo = _split(b)
    return _dot(a_hi, b_hi) + _dot(a_lo, b_hi) + _dot(a_hi, b_lo)


def _sigmoid(x):
    return 1.0 / (1.0 + jnp.exp(-x))


def _const(*_):
    return (0, 0)


def _mod_kernel(cond_ref, w_ref, b_ref, o_ref):
    c = cond_ref[...]
    s = c * _sigmoid(c)
    o_ref[...] = _dot(s.astype(BF16), w_ref[...].astype(BF16)) + b_ref[...]


def _modulation(cond, w_mod, b_mod):
    depth, d, _ = w_mod.shape
    return pl.pallas_call(
        _mod_kernel,
        grid=(depth, 6),
        in_specs=[
            pl.BlockSpec((SUBLANES, d), lambda l, j: (0, 0)),
            pl.BlockSpec((None, d, d), lambda l, j: (l, 0, j)),
            pl.BlockSpec((None, 1, d), lambda l, j: (l, 0, j)),
        ],
        out_specs=pl.BlockSpec((None, None, SUBLANES, d), lambda l, j: (l, j, 0, 0)),
        out_shape=jax.ShapeDtypeStruct((depth, 6, SUBLANES, d), F32),
        compiler_params=_params("arbitrary", "arbitrary"),
        name="modulation",
    )(cond, w_mod, b_mod.reshape(depth, 1, 6 * d))


def _mod_spec(d, tile, seq, cond_base):
    per_seq = seq // tile if seq >= tile else 0
    if cond_base == 0:
        return pl.BlockSpec((None, 6, d), lambda i: (0, 0, 0))
    return pl.BlockSpec((None, 6, d), lambda i: (cond_base + i // per_seq, 0, 0))


def _inproj_kernel(x_ref, mod_ref, n1_ref, w_ref, qn_ref, kn_ref, g_ref, cos_ref, sin_ref,
                   qt_ref, k_ref, kb_ref, v_ref, vt_ref, zc_ref, zh_ref, *, rotary):
    x = x_ref[...]
    ms = jnp.mean(x * x, axis=-1, keepdims=True)
    h = x * lax.rsqrt(ms + NORM_EPS) * n1_ref[...]
    h = h * (1.0 + mod_ref[1:2, :]) + mod_ref[0:1, :]
    proj = _dot(h.astype(BF16), w_ref[...])

    def head_norm(t, w_ref_):
        width = t.shape[1]
        msq = _dot((t * t).astype(BF16), g_ref[:width, :width])
        t = t * lax.rsqrt(msq + NORM_EPS) * w_ref_[:, :width]
        if rotary:
            lane = lax.broadcasted_iota(jnp.int32, (1, width), 1)
            partner = jnp.where((lane % 32) < 16, pltpu.roll(t, width - 16, axis=1), pltpu.roll(t, 16, axis=1))
            t = t * cos_ref[:, :width] + partner * sin_ref[:, :width]
        return t

    scale = HEAD_DIM ** -0.5 * math.log2(math.e)
    slab = g_ref.shape[0]
    for c in range(ATTN_WIDTH // slab):
        sl = slice(c * slab, (c + 1) * slab)
        qn = head_norm(proj[:, sl], qn_ref) * scale
        qt_ref[sl, :] = qn.T.astype(qt_ref.dtype)

    kn = head_norm(proj[:, ATTN_WIDTH:ATTN_WIDTH + KV_WIDTH], kn_ref)
    k_ref[...] = kn
    kb_ref[...] = kn.astype(kb_ref.dtype)
    o = ATTN_WIDTH + KV_WIDTH
    v = proj[:, o:o + KV_WIDTH]
    v_ref[...] = v
    vt_ref[...] = v.T.astype(vt_ref.dtype)
    o += KV_WIDTH
    zc_ref[...] = proj[:, o:o + zc_ref.shape[1]].astype(zc_ref.dtype)
    o += zc_ref.shape[1]
    zh_ref[...] = proj[:, o:o + zh_ref.shape[1]].astype(zh_ref.dtype)


def _rope_tables(length, width):
    pos = np.arange(length)
    rows, cols = pos // GRID_W, pos % GRID_W
    half = HEAD_DIM // 2
    inv = ROPE_BASE ** (-np.arange(0, half, 2, dtype=np.float64) / half)
    d = np.arange(HEAD_DIM)
    p = np.where(d[None, :] < half, rows[:, None], cols[:, None]).astype(np.float64)
    ang = p * inv[d % (half // 2)][None, :]
    sign = np.where((d % half) < half // 2, -1.0, 1.0)[None, :]
    cos = np.tile(np.cos(ang), (1, width // HEAD_DIM))
    sin = np.tile(np.sin(ang) * sign, (1, width // HEAD_DIM))
    return jnp.asarray(cos, F32), jnp.asarray(sin, F32)


def _group_mean_matrix(width, group):
    idx = np.arange(width) // group
    return jnp.asarray((idx[:, None] == idx[None, :]).astype(np.float32) / group, BF16)


def _inproj(x, mod_l, cond_base, norm1, w_in, q_norm, k_norm, seq, rotary, zc_w, zh_w):
    n, d = x.shape
    tm = INPROJ_TILE
    per_seq = max(seq // tm, 1)
    slab = INPROJ_SLAB
    cos, sin = _rope_tables(seq if rotary else tm, slab)
    g = _group_mean_matrix(slab, HEAD_DIM)
    tab = (lambda i: (i % per_seq, 0)) if rotary else _const
    row = lambda i: (i, 0)
    col = lambda i: (0, i)
    return pl.pallas_call(
        functools.partial(_inproj_kernel, rotary=rotary),
        grid=(n // tm,),
        in_specs=[
            pl.BlockSpec((tm, d), row),
            _mod_spec(d, tm, seq, cond_base),
            pl.BlockSpec((1, d), _const),
            pl.BlockSpec(w_in.shape, _const),
            pl.BlockSpec((1, slab), _const),
            pl.BlockSpec((1, slab), _const),
            pl.BlockSpec((slab, slab), _const),
            pl.BlockSpec((tm, slab), tab),
            pl.BlockSpec((tm, slab), tab),
        ],
        out_specs=[
            pl.BlockSpec((ATTN_WIDTH, tm), col),
            pl.BlockSpec((tm, KV_WIDTH), row),
            pl.BlockSpec((tm, KV_WIDTH), row),
            pl.BlockSpec((tm, KV_WIDTH), row),
            pl.BlockSpec((KV_WIDTH, tm), col),
            pl.BlockSpec((tm, zc_w), row),
            pl.BlockSpec((tm, zh_w), row),
        ],
        out_shape=[
            jax.ShapeDtypeStruct((ATTN_WIDTH, n), BF16),
            jax.ShapeDtypeStruct((n, KV_WIDTH), F32),
            jax.ShapeDtypeStruct((n, KV_WIDTH), BF16),
            jax.ShapeDtypeStruct((n, KV_WIDTH), F32),
            jax.ShapeDtypeStruct((KV_WIDTH, n), BF16),
            jax.ShapeDtypeStruct((n, zc_w), BF16),
            jax.ShapeDtypeStruct((n, zh_w), BF16),
        ],
        compiler_params=_params("parallel"),
        name="inproj",
    )(x, mod_l, norm1.reshape(1, d), w_in,
      jnp.tile(q_norm, slab // HEAD_DIM).reshape(1, slab),
      jnp.tile(k_norm, slab // HEAD_DIM).reshape(1, slab), g, cos, sin)


def _attn_kernel(*refs, chunks, cache_chunks):
    if cache_chunks:
        qt_ref, k_ref, vt_ref, ck_ref, cvt_ref, o_ref = refs
    else:
        qt_ref, k_ref, vt_ref, o_ref = refs
    tq = qt_ref.shape[1]
    rows = GROUP * HEAD_DIM
    heads = qt_ref.shape[0] // rows

    def ones_rows(size):
        r = lax.broadcasted_iota(jnp.int32, (BF16_SUBLANES, size), 0)
        return jnp.where(r == 0, 1.0, 0.0).astype(BF16)

    def update(s, v, m, acc):
        m_new = jnp.maximum(m, jnp.max(s, axis=0, keepdims=True))
        alpha = jnp.exp2(m - m_new)
        p = jnp.exp2(s - m_new).astype(BF16)
        v_aug = jnp.concatenate([v, ones_rows(v.shape[1])], axis=0)
        return m_new, acc * alpha + _dot(v_aug, p)

    for hh in range(heads):
        h = hh if heads == N_KV_HEADS else pl.program_id(1)
        q4 = jnp.concatenate([qt_ref[hh * rows + g * HEAD_DIM:hh * rows + (g + 1) * HEAD_DIM, :]
                              for g in range(GROUP)], axis=1)
        zero = jnp.zeros_like(q4)
        q_pad = jnp.where(h == 0, jnp.concatenate([q4, zero], axis=0), jnp.concatenate([zero, q4], axis=0))
        vrows = slice(hh * HEAD_DIM, (hh + 1) * HEAD_DIM)

        m = jnp.full((1, GROUP * tq), -1e30, F32)
        acc = jnp.zeros((HEAD_DIM + BF16_SUBLANES, GROUP * tq), F32)
        for start, size in chunks:
            m, acc = update(_dot(k_ref[start:start + size, :], q_pad), vt_ref[vrows, start:start + size], m, acc)
        for start, size in cache_chunks:
            m, acc = update(_dot(ck_ref[start:start + size, :], q_pad), cvt_ref[vrows, start:start + size], m, acc)
        out = acc[:HEAD_DIM] / acc[HEAD_DIM:HEAD_DIM + 1]
        out = jnp.concatenate([out[:, g * tq:(g + 1) * tq] for g in range(GROUP)], axis=0)
        o_ref[:, hh * rows:(hh + 1) * rows] = out.T.astype(o_ref.dtype)


def _chunks(n):
    return [(s, min(ATTN_K_CHUNK, n - s)) for s in range(0, n, ATTN_K_CHUNK)]


def _attention(qt, kb, vt, seq, cache=None):
    n = kb.shape[0]
    nb = n // seq
    tq = min(ATTN_Q_TILE, seq)
    nq = seq // tq
    hps = N_KV_HEADS if seq <= ATTN_K_CHUNK else 1
    rows = hps * GROUP * HEAD_DIM
    in_specs = [
        pl.BlockSpec((rows, tq), lambda b, h, i: (h, b * nq + i)),
        pl.BlockSpec((seq, KV_WIDTH), lambda b, h, i: (b, 0)),
        pl.BlockSpec((hps * HEAD_DIM, seq), lambda b, h, i: (h, b)),
    ]
    args = [qt, kb, vt]
    cache_chunks = []
    if cache is not None:
        past = cache[0].shape[1]
        in_specs += [
            pl.BlockSpec((None, past, KV_WIDTH), lambda b, h, i: (b, 0, 0)),
            pl.BlockSpec((None, hps * HEAD_DIM, past), lambda b, h, i: (b, h, 0)),
        ]
        args += list(cache)
        cache_chunks = _chunks(past)
    return pl.pallas_call(
        functools.partial(_attn_kernel, chunks=_chunks(seq), cache_chunks=cache_chunks),
        grid=(nb, N_KV_HEADS // hps, nq),
        in_specs=in_specs,
        out_specs=pl.BlockSpec((tq, rows), lambda b, h, i: (b * nq + i, h)),
        out_shape=jax.ShapeDtypeStruct((n, ATTN_WIDTH), BF16),
        compiler_params=_params("parallel", "parallel", "parallel"),
        name="attention",
    )(*args)


def _conformer_kernel(z_ref, zp_ref, zn_ref, dww_ref, dwb_ref, gnw_ref, gnb_ref, g_ref, pww_ref, pwb_ref,
                      o_ref, ubuf_ref, cbuf_ref, *, tiles_per_seq):
    i = pl.program_id(0)
    tt, c = o_ref.shape
    halo = CONV_HALO
    first = (i % tiles_per_seq) == 0
    last = (i % tiles_per_seq) == tiles_per_seq - 1

    def glu(ref):
        z = ref[...].astype(F32)
        return z[:, :c] * _sigmoid(z[:, c:])

    ubuf_ref[0, 0:halo, :] = jnp.where(first, 0.0, glu(zp_ref))
    ubuf_ref[0, halo:halo + tt, :] = glu(z_ref)
    ubuf_ref[0, halo + tt:, :] = jnp.where(last, 0.0, glu(zn_ref))
    span = tt + 2 * halo - SUBLANES
    for s in range(1, SUBLANES):
        ubuf_ref[s, 0:span, :] = ubuf_ref[0, s:s + span, :]

    pad = (CONV_KSIZE - 1) // 2
    rows = 64
    for r in range(tt // rows):
        acc = jnp.zeros((rows, c), F32) + dwb_ref[...]
        for k in range(CONV_KSIZE):
            off = halo + r * rows - pad + k
            s = off % SUBLANES
            acc = acc + ubuf_ref[s, off - s:off - s + rows, :] * dww_ref[k:k + 1, :]
        cbuf_ref[r * rows:(r + 1) * rows, :] = acc

    g = g_ref[...]
    u = cbuf_ref[...]
    hi, lo = _split(u)
    mu = _dot(hi, g) + _dot(lo, g)
    dlt = u - mu
    var = _dot((dlt * dlt).astype(BF16), g)
    un = dlt * lax.rsqrt(var + NORM_EPS) * gnw_ref[...] + gnb_ref[...]
    act = un * _sigmoid(un)
    o_ref[...] = (_dot(act.astype(BF16), pww_ref[...]) + pwb_ref[...]).astype(o_ref.dtype)


def _conformer(zc, dw_w, dw_b, gn_w, gn_b, pw_w, pw_b, seq):
    n, c2 = zc.shape
    c = c2 // 2
    tt = min(CONV_TILE, seq)
    hb = tt // CONV_HALO
    n_halo_blocks = n // CONV_HALO
    dw_w = jnp.concatenate([dw_w, jnp.zeros((1, c), F32)], axis=0)
    g = _group_mean_matrix(c, CONV_GROUP_WIDTH)
    return pl.pallas_call(
        functools.partial(_conformer_kernel, tiles_per_seq=seq // tt),
        grid=(n // tt,),
        in_specs=[
            pl.BlockSpec((tt, c2), lambda i: (i, 0)),
            pl.BlockSpec((CONV_HALO, c2), lambda i: (jnp.maximum(i * hb - 1, 0), 0)),
            pl.BlockSpec((CONV_HALO, c2), lambda i: (jnp.minimum((i + 1) * hb, n_halo_blocks - 1), 0)),
            pl.BlockSpec((CONV_KSIZE + 1, c), _const),
            pl.BlockSpec((1, c), _const),
            pl.BlockSpec((1, c), _const),
            pl.BlockSpec((1, c), _const),
            pl.BlockSpec((c, c), _const),
            pl.BlockSpec((c, c), _const),
            pl.BlockSpec((1, c), _const),
        ],
        out_specs=pl.BlockSpec((tt, c), lambda i: (i, 0)),
        out_shape=jax.ShapeDtypeStruct((n, c), BF16),
        scratch_shapes=[pltpu.VMEM((SUBLANES, tt + 2 * CONV_HALO, c), F32), pltpu.VMEM((tt, c), F32)],
        compiler_params=_params("parallel"),
        name="conformer",
    )(zc, zc, zc, dw_w, dw_b.reshape(1, c), gn_w.reshape(1, c), gn_b.reshape(1, c), g,
      pw_w.astype(BF16), pw_b.reshape(1, c))


def _fpad(t):
    return -(-(t + 1) // SUBLANES) * SUBLANES


def _dft_tables(t_blk):
    fpad = _fpad(t_blk)
    t = np.arange(t_blk)
    k = np.arange(fpad)
    ang = ((k[:, None] * t[None, :]) % (2 * t_blk)) * (2.0 * np.pi / (2 * t_blk))
    live = (k <= t_blk)[:, None]
    fwd = np.concatenate([np.where(live, np.cos(ang), 0.0), np.where(live, -np.sin(ang), 0.0)], axis=0)
    ck = np.where((k == 0) | (k == t_blk), 1.0, 2.0)[:, None] / (2 * t_blk)
    inv = np.concatenate([np.where(live, ck * np.cos(ang), 0.0), np.where(live, -ck * np.sin(ang), 0.0)], axis=0).T

    return jnp.asarray(fwd, BF16), jnp.asarray(inv, BF16)


def _filter_features(length):
    lag = np.arange(2 * length) - length
    idx = np.minimum(np.abs(lag), length - 1)
    t = np.linspace(0.0, 1.0, length)[idx]
    w_ang = 2.0 * np.pi * idx / length
    bands = np.linspace(1e-4, HYENA_POS_BANDS - 1, HYENA_POS_BANDS)
    z = np.zeros((2 * length, LANES), np.float64)
    z[:, 0] = t
    z[:, 1:1 + HYENA_POS_BANDS] = np.cos(bands[None, :] * w_ang[:, None])
    z[:, 1 + HYENA_POS_BANDS:1 + 2 * HYENA_POS_BANDS] = np.sin(-bands[None, :] * w_ang[:, None])
    z[:, 1 + 2 * HYENA_POS_BANDS] = (lag > -length).astype(np.float64)
    return jnp.asarray(z, F32), jnp.asarray(z.T, F32)


def _hyena_filter_kernel(z_ref, zt_ref, w1t_ref, b1_ref, w2t_ref, b2_ref, fr_ref, w3_ref, b3_ref, dl_ref,
                         fh_ref, h_ref, prev_ref):
    s = pl.program_id(0)
    fpad = prev_ref.shape[0] // 2
    z = z_ref[...]
    fr = fr_ref[...]

    def dense_t(wt_ref, a, b_ref_):
        w_hi, w_lo = _split(wt_ref[...])
        return _dot3l(w_hi, w_lo, a) + b_ref_[...]

    hid = jnp.sin(fr * dense_t(w1t_ref, zt_ref[...], b1_ref))
    hid = jnp.sin(fr * dense_t(w2t_ref, hid, b2_ref)).T
    h = _dot(hid.astype(BF16), w3_ref[...].astype(BF16)) + b3_ref[...]
    t = z[:, 0:1]
    live = z[:, 1 + 2 * HYENA_POS_BANDS:2 + 2 * HYENA_POS_BANDS]
    h = h * jnp.exp(-t * dl_ref[...]) * live
    spec = _dot(fh_ref[...], h.astype(BF16))

    @pl.when(s == 0)
    def _():
        prev_ref[...] = jnp.zeros_like(prev_ref)

    row = lax.broadcasted_iota(jnp.int32, (2 * fpad, 1), 0)
    sign = (1 - 2 * ((row % fpad) % 2)).astype(F32)
    h_ref[...] = spec + sign * prev_ref[...]
    prev_ref[...] = spec


def _hyena_filters(length, t_blk, w1, b1, w2, b2, w3, b3, freq, dft_fwd):
    nb = length // t_blk
    fpad = _fpad(t_blk)
    fh = w2.shape[0]
    oc = w3.shape[1] // 2
    c = oc // HYENA_ORDER
    z, zt = _filter_features(length)
    w1t = jnp.concatenate([w1, jnp.zeros((LANES - w1.shape[0], fh), F32)], axis=0).T
    w3d = w3.reshape(fh, HYENA_ORDER, 2, c).transpose(2, 0, 1, 3).reshape(2, fh, oc)
    b3d = b3.reshape(HYENA_ORDER, 2, c).transpose(1, 0, 2).reshape(2, 1, oc)
    deltas = np.abs(np.linspace(HYENA_MIN_DECAY, HYENA_MAX_DECAY, c))
    dl = jnp.asarray(np.tile(deltas, HYENA_ORDER)[None, :], F32)
    direction = lambda s: (jnp.where(s >= nb, 0, 1), 0, 0)
    return pl.pallas_call(
        _hyena_filter_kernel,
        grid=(2 * nb,),
        in_specs=[
            pl.BlockSpec((t_blk, LANES), lambda s: (s, 0)),
            pl.BlockSpec((LANES, t_blk), lambda s: (0, s)),
            pl.BlockSpec((fh, LANES), _const),
            pl.BlockSpec((fh, 1), _const),
            pl.BlockSpec((fh, fh), _const),
            pl.BlockSpec((fh, 1), _const),
            pl.BlockSpec((fh, 1), _const),
            pl.BlockSpec((None, fh, oc), direction),
            pl.BlockSpec((None, 1, oc), direction),
            pl.BlockSpec((1, oc), _const),
            pl.BlockSpec((2 * fpad, t_blk), _const),
        ],
        out_specs=pl.BlockSpec((None, 2 * fpad, oc), lambda s: (s, 0, 0)),
        out_shape=jax.ShapeDtypeStruct((2 * nb, 2 * fpad, oc), F32),
        scratch_shapes=[pltpu.VMEM((2 * fpad, oc), F32)],
        compiler_params=_params("arbitrary"),
        name="hyena_filter",
    )(z, zt, w1t, b1.reshape(fh, 1), w2.T, b2.reshape(fh, 1), freq.reshape(fh, 1), w3d, b3d, dl,
      dft_fwd)


def _hyena_conv_kernel(x_ref, gz_ref, sw_ref, sb_ref, gw_ref, gb_ref, bias_ref, h_ref, f_ref, g_ref,
                       o_ref, xs_ref, u_ref, y_ref, *, length, nb, short_signal, rc):
    rows, c = x_ref.shape
    t_blk = length // nb
    n_blocks = rows // t_blk
    fpad = y_ref.shape[0] // 2
    sr = min(HY_SHORT_ROWS, rows)

    def short_conv(ref, w_ref_, b_ref_, r0):
        a = ref[pl.ds(r0, sr), :].astype(F32)
        w = BF16_SUBLANES
        before = ref[pl.ds(pl.multiple_of(jnp.maximum(r0 - w, 0), w), w), :].astype(F32)[w - 1:, :]
        after = ref[pl.ds(pl.multiple_of(jnp.minimum(r0 + sr, rows - w), w), w), :].astype(F32)[:1, :]
        prev = pltpu.roll(a, 1, axis=0)
        nxt = pltpu.roll(a, sr - 1, axis=0)
        if length >= sr:
            before = jnp.where(r0 % length == 0, 0.0, before)
            after = jnp.where((r0 + sr) % length == 0, 0.0, after)
            starts = {0: before}
            ends = {sr: after}
        else:
            starts = {s: 0.0 for s in range(0, sr, length)}
            ends = {s + length: 0.0 for s in range(0, sr, length)}
        sub = lax.broadcasted_iota(jnp.int32, (SUBLANES, 1), 0)

        def patched(v, at, pick):
            pieces, done = [], 0
            for r, val in sorted(at.items()):
                g0 = (r if pick == 0 else r - SUBLANES)
                if g0 > done:
                    pieces.append(v[done:g0])
                pieces.append(jnp.where(sub == pick, val, v[g0:g0 + SUBLANES]))
                done = g0 + SUBLANES
            if done < v.shape[0]:
                pieces.append(v[done:])
            return jnp.concatenate(pieces, axis=0)

        prev = patched(prev, starts, 0)
        nxt = patched(nxt, ends, SUBLANES - 1)
        return prev * w_ref_[0:1, :] + a * w_ref_[1:2, :] + nxt * w_ref_[2:3, :] + b_ref_[...]

    def load_signal(r, _):
        r0 = pl.multiple_of(r * sr, sr)
        if short_signal:
            xs_ref[pl.ds(r0, sr), :] = short_conv(x_ref, sw_ref, sb_ref, r0)
        else:
            xs_ref[pl.ds(r0, sr), :] = x_ref[pl.ds(r0, sr), :].astype(F32)
        return 0

    lax.fori_loop(0, rows // sr, load_signal, 0)

    def fwd(j, _):
        xj = xs_ref[pl.ds(pl.multiple_of(j * t_blk, t_blk), t_blk), :]
        u_ref[j] = _dot(f_ref[...], xj.astype(BF16))
        return 0

    lax.fori_loop(0, n_blocks, fwd, 0)

    def out_block(sb, _):
        i = sb % nb
        first = sb - i

        def mac(r, _):
            re = pl.ds(pl.multiple_of(r * rc, SUBLANES), rc)
            im = pl.ds(pl.multiple_of(fpad + r * rc, SUBLANES), rc)
            are = jnp.zeros((rc, c), F32)
            aim = jnp.zeros((rc, c), F32)
            for j in range(nb):
                d = i - j + nb
                hre, him = h_ref[d, re, :], h_ref[d, im, :]
                ure, uim = u_ref[first + j, re, :], u_ref[first + j, im, :]
                are = are + hre * ure - him * uim
                aim = aim + hre * uim + him * ure
            y_ref[re, :] = are
            y_ref[im, :] = aim
            return 0

        lax.fori_loop(0, fpad // rc, mac, 0)
        blk = pl.ds(pl.multiple_of(sb * t_blk, t_blk), t_blk)
        conv = _dot(g_ref[...], y_ref[...].astype(BF16))
        xs_ref[blk, :] = conv + xs_ref[blk, :] * bias_ref[...]
        return 0

    lax.fori_loop(0, n_blocks, out_block, 0)

    def gate(r, _):
        r0 = pl.multiple_of(r * sr, sr)
        o_ref[pl.ds(r0, sr), :] = (xs_ref[pl.ds(r0, sr), :] * short_conv(gz_ref, gw_ref, gb_ref, r0)).astype(o_ref.dtype)
        return 0

    lax.fori_loop(0, rows // sr, gate, 0)


def _hyena_conv(x, x_col, zh, gate_col, short_w, short_b, bias, spectra, order, length, t_blk,
                dft_fwd, dft_inv, short_signal):
    c = bias.shape[0]
    n = zh.shape[0]
    nb = length // t_blk
    fpad = _fpad(t_blk)
    rc = max(r for r in (8, 24, 40) if fpad % r == 0)
    rows = min(max(length, HY_STEP_ROWS), n)
    big = _SINGLE if 2 * nb * 2 * fpad * c * 4 > HY_SINGLE_BUFFER_BYTES else None
    return pl.pallas_call(
        functools.partial(_hyena_conv_kernel, length=length, nb=nb, short_signal=short_signal, rc=rc),
        grid=(n // rows,),
        in_specs=[
            pl.BlockSpec((rows, c), lambda b: (b, x_col)),
            pl.BlockSpec((rows, c), lambda b: (b, gate_col)),
            pl.BlockSpec((3, c), lambda b: (0, x_col)),
            pl.BlockSpec((1, c), lambda b: (0, x_col)),
            pl.BlockSpec((3, c), lambda b: (0, gate_col)),
            pl.BlockSpec((1, c), lambda b: (0, gate_col)),
            pl.BlockSpec((1, c), lambda b: (0, 0)),
            pl.BlockSpec((2 * nb, 2 * fpad, c), lambda b: (0, 0, order), pipeline_mode=big),
            pl.BlockSpec((2 * fpad, t_blk), _const),
            pl.BlockSpec((t_blk, 2 * fpad), _const),
        ],
        out_specs=pl.BlockSpec((rows, c), lambda b: (b, 0)),
        out_shape=jax.ShapeDtypeStruct((n, c), BF16),
        scratch_shapes=[
            pltpu.VMEM((rows, c), F32),
            pltpu.VMEM((rows // t_blk, 2 * fpad, c), F32),
            pltpu.VMEM((2 * fpad, c), F32),
        ],
        compiler_params=_params("arbitrary"),
        name="hyena_conv",
    )(x, zh, short_w, short_b.reshape(1, -1), short_w, short_b.reshape(1, -1), bias.reshape(1, c), spectra,
      dft_fwd, dft_inv)


def _hyena(zh, short_w, short_b, hy_bias, filt, length):
    t_blk = min(HY_MAX_BLOCK, length)
    dft_fwd, dft_inv = _dft_tables(t_blk)
    spectra = _hyena_filters(length, t_blk, *filt, dft_fwd)
    u1 = _hyena_conv(zh, 0, zh, 1, short_w, short_b, hy_bias[0], spectra, 0, length, t_blk,
                     dft_fwd, dft_inv, True)
    return _hyena_conv(u1, 0, zh, 2, short_w, short_b, hy_bias[1], spectra, 1, length, t_blk,
                       dft_fwd, dft_inv, False)


def _mix_ffn_kernel(x_ref, xp_ref, xn_ref, a_ref, ap_ref, an_ref, c_ref, cp_ref, cn_ref, h_ref, hp_ref, hn_ref,
                    mod_ref, wo_ref, n2_ref, wup_ref, dww_ref, dwb_ref, wdn_ref, fn_ref,
                    o_ref, p_ref, act_ref, *, seq, final):
    i = pl.program_id(0)
    tm, d = x_ref.shape
    ffn = wdn_ref.shape[0]
    halo = FFN_HALO
    seg = min(seq, tm)
    nseg = tm // seg
    inside = seq > tm

    def mixed(refs, rows):
        return jnp.concatenate([r[rows, :] for r in refs], axis=1)

    mix_in = mixed((a_ref, c_ref, h_ref), slice(None))
    x_in = x_ref[...]
    if inside:
        lo, hi = slice(BF16_SUBLANES - halo, BF16_SUBLANES), slice(0, halo)
        mix_in = jnp.concatenate([mixed((ap_ref, cp_ref, hp_ref), lo), mix_in,
                                  mixed((an_ref, cn_ref, hn_ref), hi)], axis=0)
        x_in = jnp.concatenate([xp_ref[...], x_in, xn_ref[...]], axis=0)
    x1 = x_in + mod_ref[2:3, :] * _dot(mix_in, wo_ref[...])

    def norm_mod(x):
        ms = jnp.mean(x * x, axis=-1, keepdims=True)
        h = x * lax.rsqrt(ms + NORM_EPS) * n2_ref[...]
        return h * (1.0 + mod_ref[4:5, :]) + mod_ref[3:4, :]

    zeros = jnp.zeros((halo, d), F32)
    if inside:
        tiles = seq // tm
        hx = norm_mod(x1)
        before = jnp.where(i % tiles == 0, 0.0, hx[:halo])
        after = jnp.where(i % tiles == tiles - 1, 0.0, hx[halo + tm:])
        hm = hx[halo:halo + tm]
        x1 = x1[halo:halo + tm]
    else:
        before, after = zeros, zeros
        hm = norm_mod(x1)
    parts = [before]
    for s in range(nseg):
        parts += [hm[s * seg:(s + 1) * seg], after if s == nseg - 1 else zeros]
    hb = jnp.concatenate(parts, axis=0).astype(BF16)

    p_ref[0] = _dot(hb, wup_ref[:, :ffn])
    p_ref[1] = _dot(hb, wup_ref[:, ffn:])
    ext = p_ref.shape[1]

    def dwconv(which, cols):
        wcols = slice(which * ffn + cols.start, which * ffn + cols.stop)
        w = dww_ref[:, wcols]
        p = p_ref[which, :, cols]
        prev = pltpu.roll(p, 1, axis=0)
        nxt = pltpu.roll(p, ext - 1, axis=0)
        out = []
        for s in range(nseg):
            o = halo + s * (seg + halo)
            out.append(prev[o:o + seg, :] * w[0:1] + p[o:o + seg, :] * w[1:2] + nxt[o:o + seg, :] * w[2:3]
                       + dwb_ref[:, wcols])
        return out[0] if nseg == 1 else jnp.concatenate(out, axis=0)

    for c0 in range(0, ffn, FFN_STRIP):
        cols = slice(c0, c0 + FFN_STRIP)
        gate = dwconv(1, cols)
        act_ref[:, cols] = (gate * _sigmoid(gate) * dwconv(0, cols)).astype(BF16)

    out = x1 + mod_ref[5:6, :] * _dot(act_ref[...], wdn_ref[...])
    if final:
        ms = jnp.mean(out * out, axis=-1, keepdims=True)
        out = out * lax.rsqrt(ms + NORM_EPS) * fn_ref[...]
    o_ref[...] = out


def _mix_ffn(x, attn, conv, hy, mod_l, cond_base, w_o, norm2, w_up, dw_w, dw_b, w_down, final_norm, seq, final):
    n, d = x.shape
    tm = TOKEN_TILE
    ffn = w_down.shape[0]
    assert ffn % FFN_STRIP == 0

    def tile_and_halos(arr, halo):
        w = arr.shape[1]
        per_tile = tm // halo
        last = n // halo - 1
        return [pl.BlockSpec((tm, w), lambda i: (i, 0)),
                pl.BlockSpec((halo, w), lambda i: (jnp.maximum(i * per_tile - 1, 0), 0)),
                pl.BlockSpec((halo, w), lambda i: (jnp.minimum((i + 1) * per_tile, last), 0))]

    return pl.pallas_call(
        functools.partial(_mix_ffn_kernel, seq=seq, final=final),
        grid=(n // tm,),
        in_specs=[
            *tile_and_halos(x, FFN_HALO),
            *tile_and_halos(attn, BF16_SUBLANES),
            *tile_and_halos(conv, BF16_SUBLANES),
            *tile_and_halos(hy, BF16_SUBLANES),
            _mod_spec(d, tm, seq, cond_base),
            pl.BlockSpec(w_o.shape, _const, pipeline_mode=_SINGLE),
            pl.BlockSpec((1, d), _const),
            pl.BlockSpec(w_up.shape, _const, pipeline_mode=_SINGLE),
            pl.BlockSpec((3, 2 * ffn), _const),
            pl.BlockSpec((1, 2 * ffn), _const),
            pl.BlockSpec(w_down.shape, _const, pipeline_mode=_SINGLE),
            pl.BlockSpec((1, d), _const),
        ],
        out_specs=pl.BlockSpec((tm, d), lambda i: (i, 0)),
        out_shape=jax.ShapeDtypeStruct((n, d), F32),
        scratch_shapes=[
            pltpu.VMEM((2, tm + (tm // min(seq, tm) + 1) * FFN_HALO, ffn), F32),
            pltpu.VMEM((tm, ffn), BF16),
        ],
        compiler_params=_params("parallel"),
        name="mix_ffn",
    )(x, x, x, attn, attn, attn, conv, conv, conv, hy, hy, hy, mod_l, w_o, norm2.reshape(1, d), w_up, dw_w,
      dw_b.reshape(1, -1), w_down, final_norm.reshape(1, d))


def kernel(x_prompt, x_sample, cache_k, cache_v, c, c_ctx, norm1, norm2, w_mod, b_mod, w_in, q_norm, k_norm,
           conv_dw_w, conv_dw_b, conv_gn_w, conv_gn_b, conv_pw_w, conv_pw_b, hy_short_w, hy_short_b,
           hy_f_w1, hy_f_b1, hy_f_w2, hy_f_b2, hy_f_w3, hy_f_b3, hy_freq, hy_bias, w_o, ffn_w_up,
           ffn_dw_w, ffn_dw_b, ffn_w_down, final_norm):
    batch, ctx_seq, d = x_prompt.shape
    dec_batch, lat_seq, _ = x_sample.shape
    past = cache_k.shape[2]
    depth = norm1.shape[0]
    conv_w = conv_pw_w.shape[1]
    hy_w = hy_bias.shape[2]
    assert 1 + dec_batch <= SUBLANES

    cond = jnp.concatenate([c_ctx[None, :], c, jnp.zeros((SUBLANES - 1 - dec_batch, d), F32)], axis=0)
    mod = _modulation(cond, w_mod, b_mod).transpose(0, 2, 1, 3)

    xs = [x_prompt.reshape(batch * ctx_seq, d), x_sample.reshape(dec_batch * lat_seq, d)]
    seqs = [ctx_seq, lat_seq]
    cond_bases = [0, 1]
    new_k, new_v = [], []
    for l in range(depth):
        w_in_l = w_in[l].astype(BF16)
        w_o_l = w_o[l].astype(BF16)
        w_up_l = ffn_w_up[l].astype(BF16)
        w_down_l = ffn_w_down[l].astype(BF16)
        filt = (hy_f_w1[l], hy_f_b1[l], hy_f_w2[l], hy_f_b2[l], hy_f_w3[l], hy_f_b3[l], hy_freq[l])
        for path in range(2):
            x, seq, base = xs[path], seqs[path], cond_bases[path]
            qt, k, kb, v, vt, zc, zh = _inproj(x, mod[l], base, norm1[l], w_in_l, q_norm[l], k_norm[l],
                                               seq, path == 1, 2 * conv_w, (HYENA_ORDER + 1) * hy_w)
            if path == 0:
                new_k.append(k.reshape(batch, ctx_seq, N_KV_HEADS, HEAD_DIM))
                new_v.append(v.reshape(batch, ctx_seq, N_KV_HEADS, HEAD_DIM))
                cache = None
            else:
                cache = (cache_k[:, l].reshape(dec_batch, past, KV_WIDTH).astype(BF16),
                         cache_v[:, l].reshape(dec_batch, past, KV_WIDTH).transpose(0, 2, 1).astype(BF16))
            attn = _attention(qt, kb, vt, seq, cache)
            conv = _conformer(zc, conv_dw_w[l], conv_dw_b[l], conv_gn_w[l], conv_gn_b[l], conv_pw_w[l],
                              conv_pw_b[l], seq)
            hy = _hyena(zh, hy_short_w[l], hy_short_b[l], hy_bias[l], filt, seq)
            xs[path] = _mix_ffn(x, attn, conv, hy, mod[l], base, w_o_l, norm2[l], w_up_l, ffn_dw_w[l],
                                ffn_dw_b[l], w_down_l, final_norm, seq, l == depth - 1)

    y_prompt = xs[0].reshape(batch, ctx_seq, d)
    y_sample = xs[1].reshape(dec_batch, lat_seq, d)
    return y_prompt, y_sample, jnp.stack(new_k, axis=1), jnp.stack(new_v, axis=1)
```
